```python
import math
import jax, jax.numpy as jnp
from jax import lax
import numpy as np

D_MODEL = 1024
BATCH = 8
SEQ = 2048
DEPTH = 4
DEC_BATCH = 128
DEC_SEQ = 4
PAST_LEN = 2048
PAGE_SIZE = 128

N_A_LAYERS = DEPTH // 2
N_B_LAYERS = DEPTH - N_A_LAYERS
A_INNER = 2 * D_MODEL
A_HEADS = 4
A_HEAD_DIM = A_INNER // A_HEADS
A_CONV = 4
A_QKV_BLOCK = 4
A_N_QKV_BLOCKS = A_INNER // A_QKV_BLOCK
A_CHUNK = 64
B_HEADS = 16
B_HEAD_DIM = 64
B_KV_HEADS = 4
B_HPG = B_HEADS // B_KV_HEADS
B_ATT = B_HEADS * B_HEAD_DIM
N_KV_SLOTS = 6
N_CACHE_SLOTS = 4
CMP_LEN = 32
CMP_STRIDE = 16
CMP_HIDDEN = 4 * B_HEAD_DIM
SEL_LEN = 64
SEL_TOPN = 16
WINDOW = 512
Q_BLOCK = 64
FORCE_BONUS = 1.0e3
NEG = -1.0e30
EPS = 1e-6

kernel_name = 'mlstm_nsa_yoco_step'


def _rmsnorm(x, w):
    xf = x.astype(jnp.float32)
    y = xf * lax.rsqrt(jnp.mean(xf * xf, axis=-1, keepdims=True) + EPS)
    return (y * w.astype(jnp.float32)).astype(x.dtype)


def _headnorm(h, w):
    mu = jnp.mean(h, axis=-1, keepdims=True)
    var = jnp.mean(jnp.square(h - mu), axis=-1, keepdims=True)
    hn = (h - mu) * lax.rsqrt(var + EPS)
    return hn.reshape(h.shape[0], h.shape[1], -1) * w.astype(jnp.float32)


def _masked_softmax(s, mask):
    s = jnp.where(mask, s.astype(jnp.float32), NEG)
    return jnp.where(mask, jax.nn.softmax(s, axis=-1), 0.0)


def _headwise(x, w):
    b, s, _ = x.shape
    xb = x.reshape(b, s, A_N_QKV_BLOCKS, A_QKV_BLOCK)
    return jnp.einsum('bsnc,ncd->bsnd', xb, w).reshape(b, s, A_INNER)


def _mlstm_chunk(carry, inp):
    c_st, n_st, m_st = carry
    q, k, v, ig, lf = inp
    L = q.shape[2]
    b = jnp.cumsum(lf, axis=-1)
    causal = np.tril(np.ones((L, L), dtype=bool))
    d = jnp.where(causal, b[..., :, None] - b[..., None, :] + ig[..., None, :], NEG)
    m_inter = b + m_st[..., None]
    m_t = jnp.maximum(m_inter, jnp.max(d, axis=-1))
    s = jnp.einsum('bhtd,bhsd->bhts', q, k) * jnp.exp(d - m_t[..., None])
    w_inter = jnp.exp(m_inter - m_t)
    num = jnp.einsum('bhts,bhse->bhte', s, v) + w_inter[..., None] * jnp.einsum('bhtd,bhde->bhte', q, c_st)
    den = jnp.sum(s, axis=-1) + w_inter * jnp.einsum('bhtd,bhd->bht', q, n_st)
    h = num / jnp.maximum(jnp.abs(den), jnp.exp(-m_t))[..., None]
    m_new = m_t[..., -1]
    w_row = jnp.exp(b[..., -1:] - b + ig - m_new[..., None])
    decay = jnp.exp(b[..., -1] + m_st - m_new)
    c_new = decay[..., None, None] * c_st + jnp.einsum('bhs,bhsd,bhse->bhde', w_row, k, v)
    n_new = decay[..., None] * n_st + jnp.einsum('bhs,bhsd->bhd', w_row, k)
    return (c_new, n_new, m_new), h


def _mlstm_cell(q, k, v, ig, lf, c0, n0, m0, chunk):
    bsz, nh, s, dh = q.shape
    nc = s // chunk

    def split(a):
        return jnp.moveaxis(a.reshape(a.shape[:2] + (nc, chunk) + a.shape[3:]), 2, 0)

    (c, n, m), h = lax.scan(_mlstm_chunk, (c0, n0, m0), (split(q), split(k), split(v), split(ig), split(lf)))
    h = jnp.moveaxis(h, 0, 2).reshape(bsz, nh, s, dh)
    return h, c, n, m


def _mlstm_layer(x, c0, n0, m0, conv0, chunk, norm_pre, w_up, conv_w, conv_b, w_q, w_k, w_v,
                 w_if, b_if, mh_w, skip, w_down, norm_post):
    f32 = jnp.float32
    bsz, s, _ = x.shape
    h = _rmsnorm(x, norm_pre)
    up = h @ w_up
    xm, z = up[..., :A_INNER], up[..., A_INNER:]
    xpad = jnp.concatenate([conv0.astype(xm.dtype), xm], axis=1)
    xconv = conv_b + xpad[:, 0:s] * conv_w[0]
    for j in range(1, A_CONV):
        xconv = xconv + xpad[:, j:j + s] * conv_w[j]
    xc = jax.nn.silu(xconv)
    conv_new = xpad[:, s:]
    q = _headwise(xc, w_q)
    k = _headwise(xc, w_k)
    v = _headwise(xm, w_v)
    gates = (jnp.concatenate([q, k, v], axis=-1) @ w_if + b_if).astype(f32)
    ig = jnp.transpose(gates[..., :A_HEADS], (0, 2, 1))
    lf = jnp.transpose(jax.nn.log_sigmoid(gates[..., A_HEADS:]), (0, 2, 1))

    def heads(a):
        return a.reshape(bsz, s, A_HEADS, A_HEAD_DIM).transpose(0, 2, 1, 3).astype(f32)

    hcell, c, n, m = _mlstm_cell(heads(q), heads(k) * (A_HEAD_DIM ** -0.5), heads(v), ig, lf,
                                 c0.astype(f32), n0.astype(f32), m0.astype(f32), chunk)
    hn = _headnorm(hcell.transpose(0, 2, 1, 3), mh_w)
    out = ((hn + skip.astype(f32) * xc.astype(f32)) * jax.nn.silu(z.astype(f32))).astype(x.dtype)
    y = _rmsnorm(out @ w_down, norm_post)
    return x + y, c.astype(c0.dtype), n.astype(n0.dtype), m.astype(m0.dtype), conv_new.astype(conv0.dtype)


def _shared_kv(x, kv_norm, w_kv):
    bsz, s, _ = x.shape
    return (_rmsnorm(x, kv_norm) @ w_kv).reshape(bsz, s, N_KV_SLOTS, B_KV_HEADS, B_HEAD_DIM)


def _compress(rows, pos, w1, b1, w2, b2):
    bsz, t, g, dh = rows.shape
    nc = (t - CMP_LEN) // CMP_STRIDE + 1
    idx = np.arange(nc)[:, None] * CMP_STRIDE + np.arange(CMP_LEN)[None, :]
    blk = rows[:, idx] + pos[:, None, :]
    flat = blk.transpose(0, 1, 3, 2, 4).reshape(bsz, nc, g, CMP_LEN * dh)
    return jax.nn.silu(flat @ w1 + b1) @ w2 + b2


def _sel_blocks(rows):
    bsz, t, g, dh = rows.shape
    ns = -(-t // SEL_LEN)
    rows = jnp.pad(rows, ((0, 0), (0, ns * SEL_LEN - t), (0, 0), (0, 0)))
    return rows.reshape(bsz, ns, SEL_LEN, g, dh).transpose(0, 3, 1, 2, 4)


def _nsa_context(kv4, cmp_pos, cmp_w1, cmp_b1, cmp_w2, cmp_b2):
    kcmp = _compress(kv4[:, :, 0], cmp_pos[0], cmp_w1[0], cmp_b1[0], cmp_w2[0], cmp_b2[0])
    vcmp = _compress(kv4[:, :, 1], cmp_pos[1], cmp_w1[1], cmp_b1[1], cmp_w2[1], cmp_b2[1])
    return kcmp, vcmp, _sel_blocks(kv4[:, :, 2]), _sel_blocks(kv4[:, :, 3])


def _nsa_core(q, qpos, kcmp, vcmp, ks_blk, vs_blk, kwin, vwin, wpos):
    bsz, nq = q.shape[:2]
    qg = q.reshape(bsz, nq, B_KV_HEADS, B_HPG, B_HEAD_DIM)
    scale = B_HEAD_DIM ** -0.5
    nc = kcmp.shape[1]
    ns = ks_blk.shape[2]
    cmp_start = np.arange(nc) * CMP_STRIDE
    cmp_end = cmp_start + CMP_LEN - 1
    s = jnp.einsum('bqghd,bngd->bghqn', qg, kcmp) * scale
    p_cmp = _masked_softmax(s, cmp_end[None, :] <= qpos[:, None])
    o_cmp = jnp.einsum('bghqn,bngd->bqghd', p_cmp.astype(vcmp.dtype), vcmp)
    sel_start = np.arange(ns) * SEL_LEN
    cover = ((cmp_start[:, None] < sel_start[None, :] + SEL_LEN)
             & (cmp_end[:, None] >= sel_start[None, :])).astype(np.float32)
    imp = jnp.einsum('bgqn,ns->bgqs', jnp.sum(p_cmp, axis=2), cover)
    cur = qpos // SEL_LEN
    jb = np.arange(ns)
    forced = (jb[None, :] == 0) | (jb[None, :] == cur[:, None]) | (jb[None, :] == cur[:, None] - 1)
    valid = sel_start[None, :] <= qpos[:, None]
    score = jnp.where(valid, imp + jnp.where(forced, FORCE_BONUS, 0.0), NEG)
    n_top = min(SEL_TOPN, ns)
    _, idx = lax.top_k(score, n_top)
    bi = jnp.arange(bsz)[:, None, None, None]
    gi = jnp.arange(B_KV_HEADS)[None, :, None, None]
    kb = ks_blk[bi, gi, idx].reshape(bsz, B_KV_HEADS, nq, n_top * SEL_LEN, B_HEAD_DIM)
    vb = vs_blk[bi, gi, idx].reshape(bsz, B_KV_HEADS, nq, n_top * SEL_LEN, B_HEAD_DIM)
    kpos = (idx[..., None] * SEL_LEN + jnp.arange(SEL_LEN)).reshape(bsz, B_KV_HEADS, nq, n_top * SEL_LEN)
    s = jnp.einsum('bqghd,bgqkd->bghqk', qg, kb) * scale
    p = _masked_softmax(s, (kpos <= qpos[None, None, :, None])[:, :, None])
    o_sel = jnp.einsum('bghqk,bgqkd->bqghd', p.astype(vb.dtype), vb)
    s = jnp.einsum('bqghd,bkgd->bghqk', qg, kwin) * scale
    wmask = ((wpos[None, :] <= qpos[:, None]) & (wpos[None, :] > qpos[:, None] - WINDOW)
             & (wpos[None, :] >= 0))
    p = _masked_softmax(s, wmask)
    o_win = jnp.einsum('bghqk,bkgd->bqghd', p.astype(vwin.dtype), vwin)
    o = jnp.stack([o_cmp, o_sel, o_win], axis=2)
    return o.reshape(bsz, nq, 3, B_HEADS, B_HEAD_DIM)


def _nsa_prompt_attend(q, ctx, kwin_full, vwin_full):
    bsz, s, nh, dh = q.shape
    nqb = s // Q_BLOCK
    kcmp, vcmp, ksb, vsb = ctx
    pad = ((0, 0), (WINDOW, 0), (0, 0), (0, 0))
    kwp = jnp.pad(kwin_full, pad)
    vwp = jnp.pad(vwin_full, pad)
    qb = jnp.moveaxis(q.reshape(bsz, nqb, Q_BLOCK, nh, dh), 1, 0)

    def one(args):
        qi, n = args
        q0 = n * Q_BLOCK
        qpos = q0 + jnp.arange(Q_BLOCK)
        kw = lax.dynamic_slice_in_dim(kwp, q0, Q_BLOCK + WINDOW, axis=1)
        vw = lax.dynamic_slice_in_dim(vwp, q0, Q_BLOCK + WINDOW, axis=1)
        wpos = q0 - WINDOW + jnp.arange(Q_BLOCK + WINDOW)
        return _nsa_core(qi, qpos, kcmp, vcmp, ksb, vsb, kw, vw, wpos)

    o = lax.map(one, (qb, jnp.arange(nqb)))
    return jnp.moveaxis(o, 0, 1).reshape(bsz, s, 3, nh, dh)


def _nsa_layer(x, attend, norm_pre, w_in, w_out, norm_post):
    f32 = jnp.float32
    bsz, s, _ = x.shape
    p = _rmsnorm(x, norm_pre) @ w_in
    q = p[..., :B_ATT].reshape(bsz, s, B_HEADS, B_HEAD_DIM)
    z = p[..., B_ATT:4 * B_ATT].reshape(bsz, s, 3, B_ATT)
    g = jax.nn.sigmoid(p[..., 4 * B_ATT:].astype(f32)).reshape(bsz, s, 3, B_HEADS)
    o = attend(q)
    y = (o.astype(f32) * g[..., None]).reshape(bsz, s, 3, B_ATT) * jax.nn.silu(z.astype(f32))
    y = jnp.sum(y, axis=2).astype(x.dtype) @ w_out
    return x + _rmsnorm(y, norm_post)


def setup_inputs(seed: int = 0) -> dict:
    key = jax.random.key(seed)
    ks = jax.random.split(key, 40)
    f32 = jnp.float32
    n_pages = PAST_LEN // PAGE_SIZE
    n_used = DEC_BATCH * n_pages
    n_pool = n_used + n_used // 4
    wb = min(WINDOW, PAST_LEN)

    def nrm(k, shape, scale):
        return jax.random.normal(k, shape, f32) * scale

    x_prompt = nrm(ks[0], (BATCH, SEQ, D_MODEL), 1.0)
    x_sample = nrm(ks[1], (DEC_BATCH, DEC_SEQ, D_MODEL), 1.0)
    cache_kv = nrm(ks[2], (n_pool, PAGE_SIZE, N_CACHE_SLOTS, B_KV_HEADS, B_HEAD_DIM), 1.0)
    page_table = jax.random.permutation(ks[3], n_pool)[:n_used].reshape(DEC_BATCH, n_pages).astype(jnp.int32)
    cache_win = nrm(ks[4], (DEC_BATCH, wb, 2, B_KV_HEADS, B_HEAD_DIM), 1.0)
    state_C = nrm(ks[5], (N_A_LAYERS, DEC_BATCH, A_HEADS, A_HEAD_DIM, A_HEAD_DIM), 0.02)
    state_n = nrm(ks[6], (N_A_LAYERS, DEC_BATCH, A_HEADS, A_HEAD_DIM), 0.1)
    state_m = nrm(ks[7], (N_A_LAYERS, DEC_BATCH, A_HEADS), 0.5)
    state_conv = nrm(ks[8], (N_A_LAYERS, DEC_BATCH, A_CONV - 1, A_INNER), 1.0)

    a_norm_pre = 1.0 + nrm(ks[9], (N_A_LAYERS, D_MODEL), 0.02)
    a_w_up = nrm(ks[10], (N_A_LAYERS, D_MODEL, 2 * A_INNER), D_MODEL ** -0.5)
    a_conv_w = nrm(ks[11], (N_A_LAYERS, A_CONV, A_INNER), A_CONV ** -0.5)
    a_conv_b = nrm(ks[12], (N_A_LAYERS, A_INNER), 0.02)
    a_w_q = nrm(ks[13], (N_A_LAYERS, A_N_QKV_BLOCKS, A_QKV_BLOCK, A_QKV_BLOCK), A_QKV_BLOCK ** -0.5)
    a_w_k = nrm(ks[14], (N_A_LAYERS, A_N_QKV_BLOCKS, A_QKV_BLOCK, A_QKV_BLOCK), A_QKV_BLOCK ** -0.5)
    a_w_v = nrm(ks[15], (N_A_LAYERS, A_N_QKV_BLOCKS, A_QKV_BLOCK, A_QKV_BLOCK), A_QKV_BLOCK ** -0.5)
    a_w_if = nrm(ks[16], (N_A_LAYERS, 3 * A_INNER, 2 * A_HEADS), (3 * A_INNER) ** -0.5)
    a_b_if = jnp.concatenate([nrm(ks[17], (N_A_LAYERS, A_HEADS), 0.1),
                              3.0 + 3.0 * jax.random.uniform(ks[18], (N_A_LAYERS, A_HEADS), f32)], axis=-1)
    a_mh_w = 1.0 + nrm(ks[19], (N_A_LAYERS, A_INNER), 0.02)
    a_skip = 1.0 + nrm(ks[20], (N_A_LAYERS, A_INNER), 0.02)
    a_w_down = nrm(ks[21], (N_A_LAYERS, A_INNER, D_MODEL), A_INNER ** -0.5)
    a_norm_post = 1.0 + nrm(ks[22], (N_A_LAYERS, D_MODEL), 0.02)

    kv_norm = 1.0 + nrm(ks[23], (D_MODEL,), 0.02)
    w_kv = nrm(ks[24], (D_MODEL, N_KV_SLOTS * B_KV_HEADS * B_HEAD_DIM), D_MODEL ** -0.5)
    cmp_pos = nrm(ks[25], (2, CMP_LEN, B_HEAD_DIM), 0.1)
    cmp_w1 = nrm(ks[26], (2, CMP_LEN * B_HEAD_DIM, CMP_HIDDEN), (CMP_LEN * B_HEAD_DIM) ** -0.5)
    cmp_b1 = nrm(ks[27], (2, CMP_HIDDEN), 0.02)
    cmp_w2 = nrm(ks[28], (2, CMP_HIDDEN, B_HEAD_DIM), 1.7 * CMP_HIDDEN ** -0.5)
    cmp_b2 = nrm(ks[29], (2, B_HEAD_DIM), 0.02)

    b_norm_pre = 1.0 + nrm(ks[30], (N_B_LAYERS, D_MODEL), 0.02)
    b_w_in = nrm(ks[31], (N_B_LAYERS, D_MODEL, 4 * B_ATT + 3 * B_HEADS), D_MODEL ** -0.5)
    b_w_out = nrm(ks[32], (N_B_LAYERS, B_ATT, D_MODEL), B_ATT ** -0.5)
    b_norm_post = 1.0 + nrm(ks[33], (N_B_LAYERS, D_MODEL), 0.02)

    return {'x_prompt': x_prompt, 'x_sample': x_sample, 'cache_kv': cache_kv, 'page_table': page_table,
            'cache_win': cache_win, 'state_C': state_C, 'state_n': state_n, 'state_m': state_m,
            'state_conv': state_conv,
            'a_norm_pre': a_norm_pre, 'a_w_up': a_w_up, 'a_conv_w': a_conv_w, 'a_conv_b': a_conv_b,
            'a_w_q': a_w_q, 'a_w_k': a_w_k, 'a_w_v': a_w_v, 'a_w_if': a_w_if, 'a_b_if': a_b_if,
            'a_mh_w': a_mh_w, 'a_skip': a_skip, 'a_w_down': a_w_down, 'a_norm_post': a_norm_post,
            'kv_norm': kv_norm, 'w_kv': w_kv, 'cmp_pos': cmp_pos, 'cmp_w1': cmp_w1, 'cmp_b1': cmp_b1,
            'cmp_w2': cmp_w2, 'cmp_b2': cmp_b2,
            'b_norm_pre': b_norm_pre, 'b_w_in': b_w_in, 'b_w_out': b_w_out, 'b_norm_post': b_norm_post}


def reference(x_prompt, x_sample, cache_kv, page_table, cache_win, state_C, state_n, state_m, state_conv,
              a_norm_pre, a_w_up, a_conv_w, a_conv_b, a_w_q, a_w_k, a_w_v, a_w_if, a_b_if,
              a_mh_w, a_skip, a_w_down, a_norm_post,
              kv_norm, w_kv, cmp_pos, cmp_w1, cmp_b1, cmp_w2, cmp_b2,
              b_norm_pre, b_w_in, b_w_out, b_norm_post):
    bp, sp = x_prompt.shape[0], x_prompt.shape[1]
    bs, ds = x_sample.shape[0], x_sample.shape[1]
    past_len = page_table.shape[1] * cache_kv.shape[1]
    xp, xs = x_prompt, x_sample
    cp_l, cs_l, np_l, ns_l, mp_l, ms_l, vp_l, vs_l = [], [], [], [], [], [], [], []
    for layer in range(DEPTH):
        if layer < N_A_LAYERS:
            a = (a_norm_pre[layer], a_w_up[layer], a_conv_w[layer], a_conv_b[layer], a_w_q[layer],
                 a_w_k[layer], a_w_v[layer], a_w_if[layer], a_b_if[layer], a_mh_w[layer], a_skip[layer],
                 a_w_down[layer], a_norm_post[layer])
            c0 = jnp.zeros((bp, A_HEADS, A_HEAD_DIM, A_HEAD_DIM), state_C.dtype)
            n0 = jnp.zeros((bp, A_HEADS, A_HEAD_DIM), state_n.dtype)
            m0 = jnp.full((bp, A_HEADS), NEG, state_m.dtype)
            v0 = jnp.zeros((bp, A_CONV - 1, A_INNER), state_conv.dtype)
            xp, c1, n1, m1, v1 = _mlstm_layer(xp, c0, n0, m0, v0, A_CHUNK, *a)
            xs, c2, n2, m2, v2 = _mlstm_layer(xs, state_C[layer], state_n[layer], state_m[layer],
                                              state_conv[layer], ds, *a)
            cp_l.append(c1); np_l.append(n1); mp_l.append(m1); vp_l.append(v1)
            cs_l.append(c2); ns_l.append(n2); ms_l.append(m2); vs_l.append(v2)
        else:
            if layer == N_A_LAYERS:
                kvp = _shared_kv(xp, kv_norm, w_kv)
                kvs = _shared_kv(xs, kv_norm, w_kv)
                past = cache_kv[page_table].reshape(bs, past_len, N_CACHE_SLOTS, B_KV_HEADS, B_HEAD_DIM)
                full_s = jnp.concatenate([past, kvs[:, :, :N_CACHE_SLOTS].astype(past.dtype)], axis=1)
                ctx_p = _nsa_context(kvp[:, :, :N_CACHE_SLOTS], cmp_pos, cmp_w1, cmp_b1, cmp_w2, cmp_b2)
                ctx_s = _nsa_context(full_s, cmp_pos, cmp_w1, cmp_b1, cmp_w2, cmp_b2)
                wb = cache_win.shape[1]
                win_all = jnp.concatenate([cache_win, kvs[:, :, N_CACHE_SLOTS:].astype(cache_win.dtype)], axis=1)
                qpos_s = past_len + jnp.arange(ds)
                wpos_s = past_len - wb + jnp.arange(wb + ds)
                kv_rows_prompt = kvp[:, :, :N_CACHE_SLOTS]
                kv_rows_sample = kvs[:, :, :N_CACHE_SLOTS]
                win_prompt = kvp[:, sp - min(WINDOW, sp):, N_CACHE_SLOTS:]
                win_sample = win_all[:, ds:]
            lb = layer - N_A_LAYERS
            bw = (b_norm_pre[lb], b_w_in[lb], b_w_out[lb], b_norm_post[lb])
            xp = _nsa_layer(xp, lambda q: _nsa_prompt_attend(q, ctx_p, kvp[:, :, 4], kvp[:, :, 5]), *bw)
            xs = _nsa_layer(xs, lambda q: _nsa_core(q, qpos_s, ctx_s[0], ctx_s[1], ctx_s[2], ctx_s[3],
                                                    win_all[:, :, 0], win_all[:, :, 1], wpos_s), *bw)
    y_prompt, y_sample = xp, xs
    C_prompt, C_sample = jnp.stack(cp_l), jnp.stack(cs_l)
    n_prompt, n_sample = jnp.stack(np_l), jnp.stack(ns_l)
    m_prompt, m_sample = jnp.stack(mp_l), jnp.stack(ms_l)
    conv_prompt, conv_sample = jnp.stack(vp_l), jnp.stack(vs_l)
    return (y_prompt, y_sample, kv_rows_prompt, kv_rows_sample, win_prompt, win_sample,
            C_prompt, C_sample, n_prompt, n_sample, m_prompt, m_sample, conv_prompt, conv_sample)
```

```python
import functools

import jax
import jax.numpy as jnp
import numpy as np
from jax import lax
from jax.experimental import pallas as pl
from jax.experimental.pallas import tpu as pltpu

F32 = jnp.float32
BF16 = jnp.bfloat16

D_MODEL = 1024
A_INNER = 2048
A_HEADS = 4
A_HEAD_DIM = 512
A_CONV = 4
B_HEADS = 16
B_HEAD_DIM = 64
B_KV_HEADS = 4
B_ATT = 1024
CMP_LEN = 32
CMP_STRIDE = 16
CMP_HIDDEN = 256
SEL_LEN = 64
SEL_TOPN = 16
WINDOW = 512
PAGE = 128
FORCE_BONUS = 1.0e3
NEG = -1.0e30
EPS = 1e-6

SPAD = 16
NEW_PAD = 128
CHUNK_P = 256
V7X_VMEM_BYTES = 64 * 2**20


def _params(sem, vmem_mb):
    assert vmem_mb * 2**20 < V7X_VMEM_BYTES
    return pltpu.CompilerParams(dimension_semantics=sem, vmem_limit_bytes=vmem_mb * 2**20)


def _sigmoid(x):
    return 1.0 / (1.0 + jnp.exp(-x))


def _norm_matmul_kernel(x_ref, nw_ref, w_ref, *rest):
    o_refs, xn_ref = rest[:-1], rest[-1]

    @pl.when(pl.program_id(1) == 0)
    def _():
        x = x_ref[...]
        ms = jnp.mean(x * x, axis=-1, keepdims=True)
        xn_ref[...] = (x * lax.rsqrt(ms + EPS) * nw_ref[...]).astype(BF16)

    y = jnp.dot(xn_ref[...], w_ref[...], preferred_element_type=F32)
    for o_ref in o_refs:
        o_ref[...] = y.astype(o_ref.dtype)


def norm_matmul(x, nw, w, tm, tn, also_bf16=False):
    m, k = x.shape
    n = w.shape[1]
    assert m % tm == 0 and n % tn == 0
    o_spec = pl.BlockSpec((tm, tn), lambda i, j: (i, j))
    out_specs = [o_spec]
    out_shape = [jax.ShapeDtypeStruct((m, n), F32)]
    if also_bf16:
        out_specs.append(o_spec)
        out_shape.append(jax.ShapeDtypeStruct((m, n), BF16))
    res = pl.pallas_call(
        _norm_matmul_kernel,
        grid=(m // tm, n // tn),
        in_specs=[pl.BlockSpec((tm, k), lambda i, j: (i, 0)),
                  pl.BlockSpec((1, k), lambda i, j: (0, 0)),
                  pl.BlockSpec((k, tn), lambda i, j: (0, j))],
        out_specs=out_specs,
        out_shape=out_shape,
        scratch_shapes=[pltpu.VMEM((tm, k), BF16)],
        compiler_params=_params(("parallel", "arbitrary"), 40),
        name="norm_matmul",
    )(x, nw.reshape(1, k), w)
    return res if also_bf16 else res[0]


def _matmul_norm_res_kernel(a_ref, w_ref, nw_ref, x_ref, o_ref):
    y = jnp.dot(a_ref[...], w_ref[...], preferred_element_type=F32)
    ms = jnp.mean(y * y, axis=-1, keepdims=True)
    o_ref[...] = x_ref[...] + y * lax.rsqrt(ms + EPS) * nw_ref[...]


def matmul_norm_res(a, w, nw, x, tm):
    m, k = a.shape
    n = w.shape[1]
    assert m % tm == 0
    return pl.pallas_call(
        _matmul_norm_res_kernel,
        grid=(m // tm,),
        in_specs=[pl.BlockSpec((tm, k), lambda i: (i, 0)),
                  pl.BlockSpec((k, n), lambda i: (0, 0)),
                  pl.BlockSpec((1, n), lambda i: (0, 0)),
                  pl.BlockSpec((tm, n), lambda i: (i, 0))],
        out_specs=pl.BlockSpec((tm, n), lambda i: (i, 0)),
        out_shape=jax.ShapeDtypeStruct((m, n), F32),
        compiler_params=_params(("parallel",), 40),
        name="matmul_norm_res",
    )(a, w, nw.reshape(1, n), x)


def _mlstm_pre_kernel(xm_ref, prev_ref, cw_ref, cb_ref, cq_ref, ck_ref, cv_ref, wif_ref, bif_ref,
                      q_ref, k_ref, v_ref, xc_ref, g_ref, *rest, lt, zero_first, emit_kt):
    if emit_kt:
        kt_ref, cat_ref = rest
    else:
        (cat_ref,) = rest
    t = pl.program_id(1)
    h = pl.program_id(2)
    x = xm_ref[...]
    prev = prev_ref[...]
    if zero_first:
        prev = jnp.where(t == 0, 0.0, prev)
    cat_ref[0:8, :] = prev
    cat_ref[8:8 + lt, :] = x
    cw = cw_ref[...]
    xconv = cb_ref[...] + x * cw[A_CONV - 1:A_CONV]
    for j in range(1, A_CONV):
        xconv = xconv + cat_ref[8 - j:8 - j + lt, :] * cw[A_CONV - 1 - j:A_CONV - j]
    xc = xconv * _sigmoid(xconv)

    width = x.shape[1]

    def shifted(src, delta):
        return src if delta == 0 else pltpu.roll(src, (-delta) % width, axis=1)

    cq = cq_ref[...]
    ck = ck_ref[...]
    cv = cv_ref[...]
    q = jnp.zeros_like(x)
    k = jnp.zeros_like(x)
    v = jnp.zeros_like(x)
    for delta in range(-3, 4):
        r = delta + 3
        sc = shifted(xc, delta)
        q = q + sc * cq[r:r + 1]
        k = k + sc * ck[r:r + 1]
        v = v + shifted(x, delta) * cv[r:r + 1]

    g = (jnp.dot(q.astype(BF16), wif_ref[0], preferred_element_type=F32)
         + jnp.dot(k.astype(BF16), wif_ref[1], preferred_element_type=F32)
         + jnp.dot(v.astype(BF16), wif_ref[2], preferred_element_type=F32))

    @pl.when(h == 0)
    def _():
        g_ref[...] = bif_ref[...] + g

    @pl.when(h != 0)
    def _():
        g_ref[...] += g

    ks = k * (A_HEAD_DIM ** -0.5)
    q_ref[...] = q.astype(BF16)
    k_ref[...] = ks.astype(BF16)
    v_ref[...] = v.astype(BF16)
    xc_ref[...] = xc.astype(BF16)
    if emit_kt:
        kt_ref[0] = ks.T.astype(BF16)


def _headwise_coefs(w):
    n = w.shape[0]
    rows = []
    for delta in range(-3, 4):
        cols = []
        for d in range(4):
            c = d + delta
            cols.append(w[:, c, d] if 0 <= c < 4 else jnp.zeros((n,), w.dtype))
        rows.append(jnp.stack(cols, axis=1).reshape(4 * n))
    rows.append(jnp.zeros((4 * n,), w.dtype))
    return jnp.stack(rows)


def mlstm_pre(up, halo, nseq, ntile, lt, lw, zero_first, emit_kt):
    rows = nseq * ntile * lt
    hd = A_HEAD_DIM
    if zero_first:
        def prev_map(s, t, h):
            return (jnp.maximum((s * ntile + t) * (lt // 8) - 1, 0), h)
    else:
        def prev_map(s, t, h):
            return (s, h)
    wspec = pl.BlockSpec((8, hd), lambda s, t, h: (0, h))
    row_spec = pl.BlockSpec((lt, hd), lambda s, t, h: (s * ntile + t, h))
    out_specs = [row_spec, row_spec, row_spec, row_spec,
                 pl.BlockSpec((lt, 128), lambda s, t, h: (s * ntile + t, 0))]
    out_shape = [jax.ShapeDtypeStruct((rows, A_INNER), BF16)] * 4 + [jax.ShapeDtypeStruct((rows, 128), F32)]
    if emit_kt:
        out_specs.append(pl.BlockSpec((1, hd, lt), lambda s, t, h: (s * ntile + t, h, 0)))
        out_shape.append(jax.ShapeDtypeStruct((nseq * ntile, A_INNER, lt), BF16))
    return pl.pallas_call(
        functools.partial(_mlstm_pre_kernel, lt=lt, zero_first=zero_first, emit_kt=emit_kt),
        grid=(nseq, ntile, A_HEADS),
        in_specs=[row_spec,
                  pl.BlockSpec((8, hd), prev_map),
                  wspec,
                  pl.BlockSpec((1, hd), lambda s, t, h: (0, h)),
                  wspec, wspec, wspec,
                  pl.BlockSpec((3, hd, 128), lambda s, t, h: (0, h, 0)),
                  pl.BlockSpec((1, 128), lambda s, t, h: (0, 0))],
        out_specs=out_specs,
        out_shape=out_shape,
        scratch_shapes=[pltpu.VMEM((lt + 8, hd), F32)],
        compiler_params=_params(("parallel", "parallel", "arbitrary"), 40),
        name="mlstm_pre",
    )(up, halo, lw["conv_w8"], lw["conv_b"], lw["cq"], lw["ck"], lw["cv"], lw["wif"], lw["bif"])


def _mlstm_cell_kernel(*refs, lc, valid, has_init):
    if has_init:
        (q_ref, k_ref, kt_ref, v_ref, xc_ref, z_ref, igc_ref, fgc_ref, igr_ref, fgr_ref, mh_ref, skip_ref,
         c0_ref, n0_ref, m0_ref, o_ref, cout_ref, nout_ref, mout_ref, c_sc, n_sc, m_sc) = refs
    else:
        (q_ref, k_ref, kt_ref, v_ref, xc_ref, z_ref, igc_ref, fgc_ref, igr_ref, fgr_ref, mh_ref, skip_ref,
         o_ref, cout_ref, nout_ref, mout_ref, c_sc, n_sc, m_sc) = refs
    c = pl.program_id(2)

    @pl.when(c == 0)
    def _():
        if has_init:
            c_sc[...] = c0_ref[0, 0]
            n_sc[...] = n0_ref[0, 0]
            m_sc[...] = m0_ref[0, 0]
        else:
            c_sc[...] = jnp.zeros_like(c_sc)
            n_sc[...] = jnp.zeros_like(n_sc)
            m_sc[...] = jnp.full_like(m_sc, NEG)

    def logsig(x):
        return jnp.minimum(x, 0.0) - jnp.log(1.0 + jnp.exp(-jnp.abs(x)))

    ig_c = igc_ref[0, 0]
    ig_r = igr_ref[0, 0]
    lf_c = logsig(fgc_ref[0, 0])
    lf_r = logsig(fgr_ref[0, 0])
    row = lax.broadcasted_iota(jnp.int32, (lc, lc), 0)
    col = lax.broadcasted_iota(jnp.int32, (lc, lc), 1)
    if valid < lc:
        rid = lax.broadcasted_iota(jnp.int32, (lc, 1), 0)
        cid = lax.broadcasted_iota(jnp.int32, (1, lc), 1)
        ig_c = jnp.where(rid < valid, ig_c, NEG)
        ig_r = jnp.where(cid < valid, ig_r, NEG)
        lf_c = jnp.where(rid < valid, lf_c, 0.0)
        lf_r = jnp.where(cid < valid, lf_r, 0.0)
    causal = row >= col
    b_c = jnp.sum(jnp.where(causal, lf_r, 0.0), axis=1, keepdims=True)
    b_r = jnp.sum(jnp.where(row <= col, lf_c, 0.0), axis=0, keepdims=True)
    b_last = jnp.sum(lf_r, axis=1, keepdims=True)

    m_st = m_sc[...]
    d = jnp.where(causal, b_c - b_r + ig_r, NEG)
    m_inter = b_c + m_st
    m_t = jnp.maximum(m_inter, jnp.max(d, axis=1, keepdims=True))
    q = q_ref[...]
    s = lax.dot_general(q, k_ref[...], (((1,), (1,)), ((), ())), preferred_element_type=F32)
    s = s * jnp.exp(d - m_t)
    w_inter = jnp.exp(m_inter - m_t)
    c_st = c_sc[...]
    n_st = n_sc[...]
    v = v_ref[...]
    num = (jnp.dot(s.astype(BF16), v, preferred_element_type=F32)
           + w_inter * jnp.dot(q, c_st.astype(BF16), preferred_element_type=F32))
    qn = jnp.sum(q.astype(F32) * n_st, axis=1, keepdims=True)
    den = jnp.sum(s, axis=1, keepdims=True) + w_inter * qn
    hcell = num / jnp.maximum(jnp.abs(den), jnp.exp(-m_t))

    m_new = m_t[lc - 1:lc, :]
    w_r = jnp.exp(b_last - b_r + ig_r - m_new)
    w_c = jnp.exp(b_last - b_c + ig_c - m_new)
    decay = jnp.exp(b_last + m_st - m_new)
    kw_t = (kt_ref[0].astype(F32) * w_r).astype(BF16)
    c_new = decay * c_st + jnp.dot(kw_t, v, preferred_element_type=F32)
    n_new = decay * n_st + jnp.sum(k_ref[...].astype(F32) * w_c, axis=0, keepdims=True)
    c_sc[...] = c_new
    n_sc[...] = n_new
    m_sc[...] = m_new

    mu = jnp.mean(hcell, axis=1, keepdims=True)
    hc = hcell - mu
    var = jnp.mean(hc * hc, axis=1, keepdims=True)
    hn = hc * lax.rsqrt(var + EPS) * mh_ref[...]
    z = z_ref[...]
    o_ref[...] = ((hn + skip_ref[...] * xc_ref[...].astype(F32)) * (z * _sigmoid(z))).astype(BF16)

    @pl.when(c == pl.num_programs(2) - 1)
    def _():
        cout_ref[0, 0] = c_new
        nout_ref[0, 0] = n_new
        mout_ref[0, 0] = jnp.broadcast_to(m_new, (1, 128))


def mlstm_cell(q, k, kt, v, xc, up, gates, lw, nseq, nchunk, lc, valid, init):
    hd = A_HEAD_DIM
    nt = nseq * nchunk
    g = gates[:, :2 * A_HEADS].reshape(nt, lc, 2 * A_HEADS)
    g_col = jnp.transpose(g, (0, 2, 1))[..., None]
    g_row = jnp.transpose(g, (0, 2, 1))[:, :, None, :]
    row_spec = pl.BlockSpec((lc, hd), lambda s, h, c: (s * nchunk + c, h))
    in_specs = [row_spec, row_spec,
                pl.BlockSpec((1, hd, lc), lambda s, h, c: (s * nchunk + c, h, 0)),
                row_spec, row_spec,
                pl.BlockSpec((lc, hd), lambda s, h, c: (s * nchunk + c, A_HEADS + h)),
                pl.BlockSpec((1, 1, lc, 1), lambda s, h, c: (s * nchunk + c, h, 0, 0)),
                pl.BlockSpec((1, 1, lc, 1), lambda s, h, c: (s * nchunk + c, A_HEADS + h, 0, 0)),
                pl.BlockSpec((1, 1, 1, lc), lambda s, h, c: (s * nchunk + c, h, 0, 0)),
                pl.BlockSpec((1, 1, 1, lc), lambda s, h, c: (s * nchunk + c, A_HEADS + h, 0, 0)),
                pl.BlockSpec((1, hd), lambda s, h, c: (0, h)),
                pl.BlockSpec((1, hd), lambda s, h, c: (0, h))]
    args = [q, k, kt, v, xc, up, g_col, g_col, g_row, g_row, lw["mh_w"], lw["skip"]]
    has_init = init is not None
    if has_init:
        c0, n0, m0 = init
        in_specs += [pl.BlockSpec((1, 1, hd, hd), lambda s, h, c: (s, h, 0, 0)),
                     pl.BlockSpec((1, 1, 1, hd), lambda s, h, c: (s, h, 0, 0)),
                     pl.BlockSpec((1, 1, 1, 1), lambda s, h, c: (s, h, 0, 0))]
        args += [c0, n0.reshape(nseq, A_HEADS, 1, hd), m0.reshape(nseq, A_HEADS, 1, 1)]
    out, c_out, n_out, m_out = pl.pallas_call(
        functools.partial(_mlstm_cell_kernel, lc=lc, valid=valid, has_init=has_init),
        grid=(nseq, A_HEADS, nchunk),
        in_specs=in_specs,
        out_specs=[row_spec,
                   pl.BlockSpec((1, 1, hd, hd), lambda s, h, c: (s, h, 0, 0)),
                   pl.BlockSpec((1, 1, 1, hd), lambda s, h, c: (s, h, 0, 0)),
                   pl.BlockSpec((1, 1, 1, 128), lambda s, h, c: (s, h, 0, 0))],
        out_shape=[jax.ShapeDtypeStruct((nt * lc, A_INNER), BF16),
                   jax.ShapeDtypeStruct((nseq, A_HEADS, hd, hd), F32),
                   jax.ShapeDtypeStruct((nseq, A_HEADS, 1, hd), F32),
                   jax.ShapeDtypeStruct((nseq, A_HEADS, 1, 128), F32)],
        scratch_shapes=[pltpu.VMEM((hd, hd), F32), pltpu.VMEM((1, hd), F32), pltpu.VMEM((1, 1), F32)],
        compiler_params=_params(("parallel", "parallel", "arbitrary"), 40),
        name="mlstm_cell",
    )(*args)
    return out, c_out, n_out[:, :, 0, :], m_out[:, :, 0, 0]


def _mlstm_weights(layer, a_norm_pre, a_w_up, a_conv_w, a_conv_b, a_w_q, a_w_k, a_w_v, a_w_if, a_b_if,
                   a_mh_w, a_skip, a_w_down, a_norm_post):
    wif = a_w_if[layer].reshape(3, A_INNER, 2 * A_HEADS)
    wif = jnp.pad(wif, ((0, 0), (0, 0), (0, 128 - 2 * A_HEADS))).astype(BF16)
    return {
        "norm_pre": a_norm_pre[layer],
        "w_up": a_w_up[layer].astype(BF16),
        "conv_w8": jnp.pad(a_conv_w[layer], ((0, 8 - A_CONV), (0, 0))),
        "conv_b": a_conv_b[layer].reshape(1, A_INNER),
        "cq": _headwise_coefs(a_w_q[layer]),
        "ck": _headwise_coefs(a_w_k[layer]),
        "cv": _headwise_coefs(a_w_v[layer]),
        "wif": wif,
        "bif": jnp.pad(a_b_if[layer], (0, 128 - 2 * A_HEADS)).reshape(1, 128),
        "mh_w": a_mh_w[layer].reshape(1, A_INNER),
        "skip": a_skip[layer].reshape(1, A_INNER),
        "w_down": a_w_down[layer].astype(BF16),
        "norm_post": a_norm_post[layer],
    }


def mlstm_layer_prompt(x, lw, nseq, seq):
    lc = CHUNK_P
    nchunk = seq // lc
    up = norm_matmul(x, lw["norm_pre"], lw["w_up"], 512, 1024)
    q, k, v, xc, gates, kt = mlstm_pre(up, up, nseq, nchunk, lc, lw, zero_first=True, emit_kt=True)
    out, c, n, m = mlstm_cell(q, k, kt, v, xc, up, gates, lw, nseq, nchunk, lc, lc, None)
    x_new = matmul_norm_res(out, lw["w_down"], lw["norm_post"], x, 512)
    conv_new = up.reshape(nseq, seq, 2 * A_INNER)[:, seq - (A_CONV - 1):, :A_INNER]
    return x_new, c, n, m, conv_new


def mlstm_layer_sample(x, lw, nseq, ds, c0, n0, m0, conv0):
    up = norm_matmul(x, lw["norm_pre"], lw["w_up"], 512, 1024)
    halo = jnp.pad(conv0, ((0, 0), (8 - (A_CONV - 1), 0), (0, 0))).reshape(nseq * 8, A_INNER)
    q, k, v, xc, gates = mlstm_pre(up, halo, nseq, 1, SPAD, lw, zero_first=False, emit_kt=False)
    kt = jnp.transpose(k.reshape(nseq, SPAD, A_INNER), (0, 2, 1))
    out, c, n, m = mlstm_cell(q, k, kt, v, xc, up, gates, lw, nseq, 1, SPAD, ds, (c0, n0, m0))
    x_new = matmul_norm_res(out, lw["w_down"], lw["norm_post"], x, 512)
    xm = up.reshape(nseq, SPAD, 2 * A_INNER)[:, :ds, :A_INNER]
    conv_new = jnp.concatenate([conv0, xm], axis=1)[:, ds:]
    return x_new, c, n, m, conv_new


def _compress_kernel(pt_ref, *refs, npages):
    del pt_ref
    pages = refs[:npages]
    pos_ref, w1_ref, b1_ref, w2_ref, b2_ref, o_ref, stage_ref, acc_ref = refs[npages:]
    nblk = acc_ref.shape[1]
    ncb = stage_ref.shape[0]
    for p in range(npages):
        for cb in range(ncb):
            stage_ref[cb, p * PAGE:(p + 1) * PAGE, :] = pages[p][0, :, cb * 128:(cb + 1) * 128]
    acc_ref[...] = jnp.zeros_like(acc_ref)

    def body(l, carry):
        for cb in range(ncb):
            x = stage_ref[cb, pl.ds(l, nblk, stride=CMP_STRIDE), :]
            for gg in range(2):
                c = 2 * cb + gg
                slot = c // B_KV_HEADS
                xg = x[:, gg * B_HEAD_DIM:(gg + 1) * B_HEAD_DIM]
                for half in range(2):
                    lf = half * CMP_STRIDE + l
                    xb = (xg + pos_ref[slot, lf]).astype(BF16)
                    acc_ref[2 * c + half] += jnp.dot(xb, w1_ref[slot, lf], preferred_element_type=F32)
        return carry

    lax.fori_loop(0, CMP_STRIDE, body, 0)
    out = jnp.zeros((nblk, 2 * B_KV_HEADS * B_HEAD_DIM), F32)
    for c in range(2 * B_KV_HEADS):
        slot = c // B_KV_HEADS
        nxt = pltpu.roll(acc_ref[2 * c + 1], nblk - 1, axis=0)
        hpre = acc_ref[2 * c] + nxt + b1_ref[slot]
        hid = hpre * _sigmoid(hpre)
        out = out + jnp.dot(hid.astype(BF16), w2_ref[c], preferred_element_type=F32)
    out = out + b2_ref[...]
    rid = lax.broadcasted_iota(jnp.int32, (nblk, 1), 0)
    o_ref[0] = jnp.where(rid < nblk - 1, out, 0.0)


def compress(pages_arr, table, nb, cw):
    npages = table.shape[0] // nb
    nblk = npages * (PAGE // CMP_STRIDE)
    width = 2 * B_KV_HEADS * B_HEAD_DIM

    def page_map(b, pt, p):
        return (pt[b * npages + p], 0, 0)

    def const(shape):
        return pl.BlockSpec(shape, lambda b, pt: (0,) * len(shape))

    grid_spec = pltpu.PrefetchScalarGridSpec(
        num_scalar_prefetch=1,
        grid=(nb,),
        in_specs=[pl.BlockSpec((1, PAGE, width), functools.partial(page_map, p=p)) for p in range(npages)]
        + [const((2, CMP_LEN, 1, B_HEAD_DIM)), const((2, CMP_LEN, B_HEAD_DIM, CMP_HIDDEN)),
           const((2, 1, CMP_HIDDEN)), const((2 * B_KV_HEADS, CMP_HIDDEN, width)), const((1, width))],
        out_specs=pl.BlockSpec((1, nblk, width), lambda b, pt: (b, 0, 0)),
        scratch_shapes=[pltpu.VMEM((width // 128, npages * PAGE, 128), F32),
                        pltpu.VMEM((4 * B_KV_HEADS, nblk, CMP_HIDDEN), F32)],
    )
    return pl.pallas_call(
        functools.partial(_compress_kernel, npages=npages),
        grid_spec=grid_spec,
        out_shape=jax.ShapeDtypeStruct((nb, nblk, width), F32),
        compiler_params=_params(("arbitrary",), 48),
        name="nsa_compress",
    )(table, *([pages_arr] * npages), cw["pos"], cw["w1"], cw["b1"], cw["w2p"], cw["b2p"])


def _compress_weights(cmp_pos, cmp_w1, cmp_b1, cmp_w2, cmp_b2):
    width = 2 * B_KV_HEADS * B_HEAD_DIM
    w2p = jnp.zeros((2 * B_KV_HEADS, CMP_HIDDEN, width), F32)
    for c in range(2 * B_KV_HEADS):
        w2p = w2p.at[c, :, c * B_HEAD_DIM:(c + 1) * B_HEAD_DIM].set(cmp_w2[c // B_KV_HEADS])
    return {
        "pos": cmp_pos.reshape(2, CMP_LEN, 1, B_HEAD_DIM),
        "w1": cmp_w1.reshape(2, CMP_LEN, B_HEAD_DIM, CMP_HIDDEN).astype(BF16),
        "b1": cmp_b1.reshape(2, 1, CMP_HIDDEN),
        "w2p": w2p.astype(BF16),
        "b2p": jnp.tile(cmp_b2[:, None, :], (1, B_KV_HEADS, 1)).reshape(1, width),
    }


def _softmax_masked(s, mask):
    s = jnp.where(mask, s, NEG)
    e = jnp.exp(s - jnp.max(s, axis=1, keepdims=True))
    return jnp.where(mask, e / jnp.sum(e, axis=1, keepdims=True), 0.0)


def _dot_nt(a, b):
    return lax.dot_general(a, b, (((1,), (1,)), ((), ())), preferred_element_type=F32)


def _low_half(tq):
    return lax.broadcasted_iota(jnp.int32, (tq, 128), 1) < B_HEAD_DIM


def _pad_heads(qblk, gl, tq):
    low = _low_half(tq)
    keep = low if gl == 0 else jnp.logical_not(low)
    tiles = []
    for i in range(4):
        hh = gl * 4 + i
        tile = qblk[:, (hh // 2) * 128:(hh // 2 + 1) * 128]
        if hh % 2 != gl:
            tile = pltpu.roll(tile, B_HEAD_DIM, axis=1)
        tiles.append(jnp.where(keep, tile, 0.0))
    return jnp.concatenate(tiles, axis=0).astype(BF16)


def _select_blocks(psum, qpos1, cover, ns):
    hi = psum.astype(BF16)
    lo = (psum - hi.astype(F32)).astype(BF16)
    imp = jnp.dot(hi, cover, preferred_element_type=F32) + jnp.dot(lo, cover, preferred_element_type=F32)
    s_idx = lax.broadcasted_iota(jnp.int32, (1, 128), 1)
    cur = jnp.right_shift(qpos1, 6)
    assert SEL_LEN == 64
    valid = s_idx * SEL_LEN <= qpos1
    forced = (s_idx == 0) | (s_idx == cur) | (s_idx == cur - 1)
    score = jnp.where(valid, imp + jnp.where(forced, FORCE_BONUS, 0.0), NEG)
    score = jnp.where(s_idx < ns, score, 2 * NEG)
    rank = jnp.zeros_like(score)
    for sp in range(ns):
        colv = score[:, sp:sp + 1]
        ahead = (colv > score) | ((colv == score) & (s_idx > sp))
        rank = rank + jnp.where(ahead, 1.0, 0.0)
    return jnp.where((rank < min(SEL_TOPN, ns)) & (s_idx < ns), 1.0, 0.0)


def _nsa_branches(qs, tq, qpos, qpos1, kc, vc, ks, vs, kw, vw, kpos_s, wmask, cover, expand, ns):
    n_idx = lax.broadcasted_iota(jnp.int32, (1, 128), 1)
    p_c = _softmax_masked(_dot_nt(qs, kc), n_idx * CMP_STRIDE + (CMP_LEN - 1) <= qpos)
    o_c = jnp.dot(p_c.astype(BF16), vc, preferred_element_type=F32)
    psum = p_c[0:tq] + p_c[tq:2 * tq] + p_c[2 * tq:3 * tq] + p_c[3 * tq:4 * tq]
    sel = _select_blocks(psum, qpos1, cover, ns)
    selk = jnp.dot(sel.astype(BF16), expand, preferred_element_type=F32) > 0.5
    smask = jnp.concatenate([selk] * 4, axis=0) & (kpos_s <= qpos)
    p_s = _softmax_masked(_dot_nt(qs, ks), smask)
    o_s = jnp.dot(p_s.astype(BF16), vs, preferred_element_type=F32)
    p_w = _softmax_masked(_dot_nt(qs, kw), wmask)
    o_w = jnp.dot(p_w.astype(BF16), vw, preferred_element_type=F32)
    return o_c, o_s, o_w


def _combine(o_heads, sg, zs, tq):
    low = _low_half(tq)
    y = jnp.zeros((tq, 512), F32)
    for br in range(3):
        tiles = []
        for j in range(4):
            c = br * 8 + 2 * j
            a = o_heads[br][2 * j] * sg[:, c:c + 1]
            b = o_heads[br][2 * j + 1] * sg[:, c + 1:c + 2]
            if j // 2 == 0:
                tiles.append(jnp.where(low, a, pltpu.roll(b, B_HEAD_DIM, axis=1)))
            else:
                tiles.append(jnp.where(low, pltpu.roll(a, B_HEAD_DIM, axis=1), b))
        z = zs[br]
        y = y + jnp.concatenate(tiles, axis=1) * (z * _sigmoid(z))
    return y


def _nsa_prompt_kernel(q_ref, z0_ref, z1_ref, z2_ref, gate_ref, kc_ref, vc_ref, ks_ref, vs_ref, kw_ref, vw_ref,
                       cover_ref, expand_ref, y_ref, *, tq, seq):
    q0 = pl.program_id(2) * tq
    qblk = q_ref[...] * (B_HEAD_DIM ** -0.5)
    qpos1 = q0 + lax.broadcasted_iota(jnp.int32, (tq, 1), 0)
    qpos = jnp.concatenate([qpos1] * 4, axis=0)
    wlen = WINDOW + tq
    wstart = pl.multiple_of(jnp.maximum(q0 - WINDOW, 0), 128)
    kw = kw_ref[pl.ds(wstart, wlen), :]
    vw = vw_ref[pl.ds(wstart, wlen), :]
    kpos_w = wstart + lax.broadcasted_iota(jnp.int32, (1, wlen), 1)
    wmask = (kpos_w <= qpos) & (kpos_w > qpos - WINDOW)
    kpos_s = lax.broadcasted_iota(jnp.int32, (1, seq), 1)
    kc = kc_ref[0].astype(BF16)
    vc = vc_ref[0].astype(BF16)
    o_heads = [[None] * 8 for _ in range(3)]
    for gl in range(2):
        qs = _pad_heads(qblk, gl, tq)
        outs = _nsa_branches(qs, tq, qpos, qpos1, kc, vc, ks_ref[...], vs_ref[...], kw, vw, kpos_s, wmask,
                             cover_ref[...], expand_ref[...], seq // SEL_LEN)
        for br in range(3):
            for i in range(4):
                o_heads[br][gl * 4 + i] = outs[br][i * tq:(i + 1) * tq]
    y = _combine(o_heads, _sigmoid(gate_ref[...]), (z0_ref[...], z1_ref[...], z2_ref[...]), tq)
    y_ref[...] = y.astype(BF16)


def _cover_matrix():
    n = np.arange(128)[:, None]
    s = np.arange(128)[None, :]
    cov = (n * CMP_STRIDE < s * SEL_LEN + SEL_LEN) & (n * CMP_STRIDE + CMP_LEN - 1 >= s * SEL_LEN)
    return jnp.asarray(cov, BF16)


def _expand_matrix(nkeys):
    s = np.arange(128)[:, None]
    key = np.arange(nkeys)[None, :]
    return jnp.asarray(key // SEL_LEN == s, BF16)


def nsa_attend_prompt(pm, pg, cmp, kvb, nb, seq):
    tq = 128
    nq = seq // tq

    def rows(width, col):
        return pl.BlockSpec((tq, width), lambda b, gp, t: (b * nq + t, col(gp)))

    def seq_block(first):
        return pl.BlockSpec((seq, 128), lambda b, gp, t: (b, first + gp))

    return pl.pallas_call(
        functools.partial(_nsa_prompt_kernel, tq=tq, seq=seq),
        grid=(nb, 2, nq),
        in_specs=[rows(512, lambda gp: gp), rows(512, lambda gp: 2 + gp), rows(512, lambda gp: 4 + gp),
                  rows(512, lambda gp: 6 + gp), rows(128, lambda gp: gp),
                  pl.BlockSpec((1, 128, 128), lambda b, gp, t: (b, 0, gp)),
                  pl.BlockSpec((1, 128, 128), lambda b, gp, t: (b, 0, 2 + gp)),
                  seq_block(4), seq_block(6), seq_block(8), seq_block(10),
                  pl.BlockSpec((128, 128), lambda b, gp, t: (0, 0)),
                  pl.BlockSpec((128, seq), lambda b, gp, t: (0, 0))],
        out_specs=rows(512, lambda gp: gp),
        out_shape=jax.ShapeDtypeStruct((nb * seq, B_ATT), BF16),
        compiler_params=_params(("parallel", "parallel", "arbitrary"), 56),
        name="nsa_attend_prompt",
    )(pm, pm, pm, pm, pg, cmp, cmp, kvb, kvb, kvb, kvb, _cover_matrix(), _expand_matrix(seq))


def _nsa_sample_kernel(pt_ref, *refs, npages, past):
    del pt_ref
    pages = refs[:npages]
    (q_ref, z0_ref, z1_ref, z2_ref, gate_ref, cmp_ref, new_ref, win_ref, cover_ref, expand_ref, y_ref,
     ks_sc, vs_sc, kw_sc, vw_sc) = refs[npages:]
    tq = SPAD
    wrows = win_ref.shape[1]
    qpos1 = past + lax.broadcasted_iota(jnp.int32, (tq, 1), 0)
    qpos = jnp.concatenate([qpos1] * 4, axis=0)
    kpos_s = lax.broadcasted_iota(jnp.int32, (1, past + NEW_PAD), 1)
    kpos_w = (past - wrows) + lax.broadcasted_iota(jnp.int32, (1, wrows + NEW_PAD), 1)
    wmask = (kpos_w <= qpos) & (kpos_w > qpos - WINDOW) & (kpos_w >= 0)
    sg = _sigmoid(gate_ref[...])
    zpad = jnp.zeros((NEW_PAD - tq, 128), BF16)
    ks_sc[past + tq:past + NEW_PAD, :] = zpad
    vs_sc[past + tq:past + NEW_PAD, :] = zpad
    kw_sc[wrows + tq:wrows + NEW_PAD, :] = zpad
    vw_sc[wrows + tq:wrows + NEW_PAD, :] = zpad
    for gp in range(2):
        lo, hi = gp * 128, (gp + 1) * 128
        for p in range(npages):
            ks_sc[p * PAGE:(p + 1) * PAGE, :] = pages[p][0, :, lo:hi].astype(BF16)
            vs_sc[p * PAGE:(p + 1) * PAGE, :] = pages[p][0, :, 256 + lo:256 + hi].astype(BF16)
        ks_sc[past:past + tq, :] = new_ref[:, 512 + lo:512 + hi].astype(BF16)
        vs_sc[past:past + tq, :] = new_ref[:, 768 + lo:768 + hi].astype(BF16)
        kw_sc[0:wrows, :] = win_ref[0, :, lo:hi].astype(BF16)
        vw_sc[0:wrows, :] = win_ref[0, :, 256 + lo:256 + hi].astype(BF16)
        kw_sc[wrows:wrows + tq, :] = new_ref[:, 1024 + lo:1024 + hi].astype(BF16)
        vw_sc[wrows:wrows + tq, :] = new_ref[:, 1280 + lo:1280 + hi].astype(BF16)
        kc = cmp_ref[0, :, lo:hi].astype(BF16)
        vc = cmp_ref[0, :, 256 + lo:256 + hi].astype(BF16)
        qblk = q_ref[:, gp * 512:(gp + 1) * 512] * (B_HEAD_DIM ** -0.5)
        o_heads = [[None] * 8 for _ in range(3)]
        for gl in range(2):
            qs = _pad_heads(qblk, gl, tq)
            outs = _nsa_branches(qs, tq, qpos, qpos1, kc, vc, ks_sc[...], vs_sc[...], kw_sc[...], vw_sc[...],
                                 kpos_s, wmask, cover_ref[...], expand_ref[...], past // SEL_LEN + 1)
            for br in range(3):
                for i in range(4):
                    o_heads[br][gl * 4 + i] = outs[br][i * tq:(i + 1) * tq]
        zs = (z0_ref[:, gp * 512:(gp + 1) * 512], z1_ref[:, gp * 512:(gp + 1) * 512],
              z2_ref[:, gp * 512:(gp + 1) * 512])
        y_ref[:, gp * 512:(gp + 1) * 512] = _combine(o_heads, sg[:, lo:hi], zs, tq).astype(BF16)


def nsa_attend_sample(pm, pg, cmp, kv_new, cache_pages, table, cache_win, nb, past):
    npages = past // PAGE
    wrows = cache_win.shape[1]
    nks = past + NEW_PAD
    assert past % SEL_LEN == 0 and SPAD <= SEL_LEN

    def page_map(b, pt, p):
        return (pt[b * npages + p], 0, 1)

    def rows(width, col):
        return pl.BlockSpec((SPAD, width), lambda b, pt: (b, col))

    grid_spec = pltpu.PrefetchScalarGridSpec(
        num_scalar_prefetch=1,
        grid=(nb,),
        in_specs=[pl.BlockSpec((1, PAGE, 512), functools.partial(page_map, p=p)) for p in range(npages)]
        + [rows(B_ATT, 0), rows(B_ATT, 1), rows(B_ATT, 2), rows(B_ATT, 3), rows(256, 0),
           pl.BlockSpec((1, 128, 512), lambda b, pt: (b, 0, 0)),
           rows(1536, 0),
           pl.BlockSpec((1, wrows, 512), lambda b, pt: (b, 0, 0)),
           pl.BlockSpec((128, 128), lambda b, pt: (0, 0)),
           pl.BlockSpec((128, nks), lambda b, pt: (0, 0))],
        out_specs=rows(B_ATT, 0),
        scratch_shapes=[pltpu.VMEM((nks, 128), BF16), pltpu.VMEM((nks, 128), BF16),
                        pltpu.VMEM((wrows + NEW_PAD, 128), BF16), pltpu.VMEM((wrows + NEW_PAD, 128), BF16)],
    )
    return pl.pallas_call(
        functools.partial(_nsa_sample_kernel, npages=npages, past=past),
        grid_spec=grid_spec,
        out_shape=jax.ShapeDtypeStruct((nb * SPAD, B_ATT), BF16),
        compiler_params=_params(("arbitrary",), 48),
        name="nsa_attend_sample",
    )(table, *([cache_pages] * npages), pm, pm, pm, pm, pg, cmp, kv_new, cache_win,
      _cover_matrix(), _expand_matrix(nks))


def _nsa_weights(lb, b_norm_pre, b_w_in, b_w_out, b_norm_post):
    w_in = b_w_in[lb]
    wg = w_in[:, 4 * B_ATT:].reshape(D_MODEL, 3, 2, 8)
    wg = jnp.transpose(wg, (0, 2, 1, 3)).reshape(D_MODEL, 2, 24)
    wg = jnp.pad(wg, ((0, 0), (0, 0), (0, 128 - 24))).reshape(D_MODEL, 256)
    return {"norm_pre": b_norm_pre[lb], "w_main": w_in[:, :4 * B_ATT].astype(BF16), "w_gate": wg.astype(BF16),
            "w_out": b_w_out[lb].astype(BF16), "norm_post": b_norm_post[lb]}


def kernel(x_prompt, x_sample, cache_kv, page_table, cache_win, state_C, state_n, state_m, state_conv,
           a_norm_pre, a_w_up, a_conv_w, a_conv_b, a_w_q, a_w_k, a_w_v, a_w_if, a_b_if,
           a_mh_w, a_skip, a_w_down, a_norm_post,
           kv_norm, w_kv, cmp_pos, cmp_w1, cmp_b1, cmp_w2, cmp_b2,
           b_norm_pre, b_w_in, b_w_out, b_norm_post):
    bp, sp = x_prompt.shape[0], x_prompt.shape[1]
    bs, ds = x_sample.shape[0], x_sample.shape[1]
    n_a = state_C.shape[0]
    n_b = b_w_in.shape[0]
    past = page_table.shape[1] * cache_kv.shape[1]
    assert cache_kv.shape[1] == PAGE and sp % PAGE == 0 and ds <= SPAD
    assert (past + ds - CMP_LEN) // CMP_STRIDE + 1 == past // CMP_STRIDE - 1
    xp = x_prompt.reshape(bp * sp, D_MODEL)
    xs = jnp.pad(x_sample, ((0, 0), (0, SPAD - ds), (0, 0))).reshape(bs * SPAD, D_MODEL)

    cp_l, cs_l, np_l, ns_l, mp_l, ms_l, vp_l, vs_l = [], [], [], [], [], [], [], []
    for layer in range(n_a):
        lw = _mlstm_weights(layer, a_norm_pre, a_w_up, a_conv_w, a_conv_b, a_w_q, a_w_k, a_w_v, a_w_if, a_b_if,
                            a_mh_w, a_skip, a_w_down, a_norm_post)
        xp, c1, n1, m1, v1 = mlstm_layer_prompt(xp, lw, bp, sp)
        xs, c2, n2, m2, v2 = mlstm_layer_sample(xs, lw, bs, ds, state_C[layer], state_n[layer], state_m[layer],
                                                state_conv[layer])
        cp_l.append(c1); np_l.append(n1); mp_l.append(m1); vp_l.append(v1)
        cs_l.append(c2); ns_l.append(n2); ms_l.append(m2); vs_l.append(v2)

    nkv = w_kv.shape[1]
    w_kv_b = w_kv.astype(BF16)
    kvp, kvp_b = norm_matmul(xp, kv_norm, w_kv_b, 512, 512, also_bf16=True)
    kvs = norm_matmul(xs, kv_norm, w_kv_b, 512, 512)
    cw = _compress_weights(cmp_pos, cmp_w1, cmp_b1, cmp_w2, cmp_b2)
    table_p = jnp.arange(bp * (sp // PAGE), dtype=jnp.int32)
    table_s = page_table.reshape(-1).astype(jnp.int32)
    cache_pages = cache_kv.reshape(cache_kv.shape[0], PAGE, 4 * B_KV_HEADS * B_HEAD_DIM)
    cmp_p = compress(kvp.reshape(bp * sp // PAGE, PAGE, nkv), table_p, bp, cw)
    cmp_s = compress(cache_pages, table_s, bs, cw)
    wb = cache_win.shape[1]
    win2 = cache_win.reshape(bs, wb, 2 * B_KV_HEADS * B_HEAD_DIM)

    for lb in range(n_b):
        bw = _nsa_weights(lb, b_norm_pre, b_w_in, b_w_out, b_norm_post)
        pm = norm_matmul(xp, bw["norm_pre"], bw["w_main"], 512, 1024)
        pg = norm_matmul(xp, bw["norm_pre"], bw["w_gate"], 512, 256)
        yp = nsa_attend_prompt(pm, pg, cmp_p, kvp_b, bp, sp)
        xp = matmul_norm_res(yp, bw["w_out"], bw["norm_post"], xp, 512)
        pm = norm_matmul(xs, bw["norm_pre"], bw["w_main"], 512, 1024)
        pg = norm_matmul(xs, bw["norm_pre"], bw["w_gate"], 512, 256)
        ys = nsa_attend_sample(pm, pg, cmp_s, kvs, cache_pages, table_s, win2, bs, past)
        xs = matmul_norm_res(ys, bw["w_out"], bw["norm_post"], xs, 512)

    y_prompt = xp.reshape(bp, sp, D_MODEL)
    y_sample = xs.reshape(bs, SPAD, D_MODEL)[:, :ds]
    kvp3 = kvp.reshape(bp, sp, nkv)
    kvs3 = kvs.reshape(bs, SPAD, nkv)[:, :ds]
    ncache = 4 * B_KV_HEADS * B_HEAD_DIM
    kv_rows_prompt = kvp3[:, :, :ncache].reshape(bp, sp, 4, B_KV_HEADS, B_HEAD_DIM)
    kv_rows_sample = kvs3[:, :, :ncache].reshape(bs, ds, 4, B_KV_HEADS, B_HEAD_DIM)
    win_prompt = kvp3[:, sp - min(WINDOW, sp):, ncache:].reshape(bp, min(WINDOW, sp), 2, B_KV_HEADS, B_HEAD_DIM)
    win_new = kvs3[:, :, ncache:].reshape(bs, ds, 2, B_KV_HEADS, B_HEAD_DIM).astype(cache_win.dtype)
    win_sample = jnp.concatenate([cache_win, win_new], axis=1)[:, ds:]
    return (y_prompt, y_sample, kv_rows_prompt, kv_rows_sample, win_prompt, win_sample,
            jnp.stack(cp_l), jnp.stack(cs_l), jnp.stack(np_l), jnp.stack(ns_l),
            jnp.stack(mp_l), jnp.stack(ms_l), jnp.stack(vp_l), jnp.stack(vs_l))
```

```python
import functools

import jax
import jax.numpy as jnp
import numpy as np
from jax import lax
from jax.experimental import pallas as pl
from jax.experimental.pallas import tpu as pltpu

F32 = jnp.float32
BF16 = jnp.bfloat16

D_MODEL = 1024
A_INNER = 2048
A_HEADS = 4
A_HEAD_DIM = 512
A_CONV = 4
B_HEADS = 16
B_HEAD_DIM = 64
B_KV_HEADS = 4
B_ATT = 1024
CMP_LEN = 32
CMP_STRIDE = 16
CMP_HIDDEN = 256
SEL_LEN = 64
SEL_TOPN = 16
WINDOW = 512
PAGE = 128
FORCE_BONUS = 1.0e3
NEG = -1.0e30
EPS = 1e-6

SPAD = 16
NEW_PAD = 128
CHUNK_P = 256
V7X_VMEM_BYTES = 64 * 2**20


def _params(sem, vmem_mb):
    assert vmem_mb * 2**20 < V7X_VMEM_BYTES
    return pltpu.CompilerParams(dimension_semantics=sem, vmem_limit_bytes=vmem_mb * 2**20)


def _sigmoid(x):
    return 1.0 / (1.0 + jnp.exp(-x))


def _norm_matmul_kernel(x_ref, nw_ref, w_ref, *rest):
    o_refs, xn_ref = rest[:-1], rest[-1]

    @pl.when(pl.program_id(1) == 0)
    def _():
        x = x_ref[...]
        ms = jnp.mean(x * x, axis=-1, keepdims=True)
        xn_ref[...] = (x * lax.rsqrt(ms + EPS) * nw_ref[...]).astype(BF16)

    y = jnp.dot(xn_ref[...], w_ref[...], preferred_element_type=F32)
    for o_ref in o_refs:
        o_ref[...] = y.astype(o_ref.dtype)


def norm_matmul(x, nw, w, tm, tn, also_bf16=False):
    m, k = x.shape
    n = w.shape[1]
    assert m % tm == 0 and n % tn == 0
    o_spec = pl.BlockSpec((tm, tn), lambda i, j: (i, j))
    out_specs = [o_spec]
    out_shape = [jax.ShapeDtypeStruct((m, n), F32)]
    if also_bf16:
        out_specs.append(o_spec)
        out_shape.append(jax.ShapeDtypeStruct((m, n), BF16))
    res = pl.pallas_call(
        _norm_matmul_kernel,
        grid=(m // tm, n // tn),
        in_specs=[pl.BlockSpec((tm, k), lambda i, j: (i, 0)),
                  pl.BlockSpec((1, k), lambda i, j: (0, 0)),
                  pl.BlockSpec((k, tn), lambda i, j: (0, j))],
        out_specs=out_specs,
        out_shape=out_shape,
        scratch_shapes=[pltpu.VMEM((tm, k), BF16)],
        compiler_params=_params(("parallel", "arbitrary"), 40),
        name="norm_matmul",
    )(x, nw.reshape(1, k), w)
    return res if also_bf16 else res[0]


def _matmul_norm_res_kernel(a_ref, w_ref, nw_ref, x_ref, o_ref):
    y = jnp.dot(a_ref[...], w_ref[...], preferred_element_type=F32)
    ms = jnp.mean(y * y, axis=-1, keepdims=True)
    o_ref[...] = x_ref[...] + y * lax.rsqrt(ms + EPS) * nw_ref[...]


def matmul_norm_res(a, w, nw, x, tm):
    m, k = a.shape
    n = w.shape[1]
    assert m % tm == 0
    return pl.pallas_call(
        _matmul_norm_res_kernel,
        grid=(m // tm,),
        in_specs=[pl.BlockSpec((tm, k), lambda i: (i, 0)),
                  pl.BlockSpec((k, n), lambda i: (0, 0)),
                  pl.BlockSpec((1, n), lambda i: (0, 0)),
                  pl.BlockSpec((tm, n), lambda i: (i, 0))],
        out_specs=pl.BlockSpec((tm, n), lambda i: (i, 0)),
        out_shape=jax.ShapeDtypeStruct((m, n), F32),
        compiler_params=_params(("parallel",), 40),
        name="matmul_norm_res",
    )(a, w, nw.reshape(1, n), x)


def _mlstm_pre_kernel(xm_ref, prev_ref, cw_ref, cb_ref, cq_ref, ck_ref, cv_ref, wif_ref, bif_ref,
                      q_ref, k_ref, v_ref, xc_ref, g_ref, *rest, lsub, nsub, zero_first, emit_kt):
    if emit_kt:
        kt_ref, cat_ref = rest
    else:
        (cat_ref,) = rest
    t = pl.program_id(1)
    h = pl.program_id(2)
    x = xm_ref[...]
    prev = prev_ref[...]
    if zero_first:
        prev = jnp.where(t == 0, 0.0, prev)
    cw = cw_ref[...]
    pieces = []
    for s in range(nsub):
        base = s * (lsub + 8)
        xs = x[s * lsub:(s + 1) * lsub]
        cat_ref[base:base + 8, :] = prev[s * 8:(s + 1) * 8]
        cat_ref[base + 8:base + 8 + lsub, :] = xs
        acc = cb_ref[...] + xs * cw[A_CONV - 1:A_CONV]
        for j in range(1, A_CONV):
            acc = acc + cat_ref[base + 8 - j:base + 8 - j + lsub, :] * cw[A_CONV - 1 - j:A_CONV - j]
        pieces.append(acc)
    xconv = pieces[0] if nsub == 1 else jnp.concatenate(pieces, axis=0)
    xc = xconv * _sigmoid(xconv)

    width = x.shape[1]

    def shifted(src, delta):
        return src if delta == 0 else pltpu.roll(src, (-delta) % width, axis=1)

    cq = cq_ref[...]
    ck = ck_ref[...]
    cv = cv_ref[...]
    q = jnp.zeros_like(x)
    k = jnp.zeros_like(x)
    v = jnp.zeros_like(x)
    for delta in range(-3, 4):
        r = delta + 3
        sc = shifted(xc, delta)
        q = q + sc * cq[r:r + 1]
        k = k + sc * ck[r:r + 1]
        v = v + shifted(x, delta) * cv[r:r + 1]

    g = (jnp.dot(q.astype(BF16), wif_ref[0], preferred_element_type=F32)
         + jnp.dot(k.astype(BF16), wif_ref[1], preferred_element_type=F32)
         + jnp.dot(v.astype(BF16), wif_ref[2], preferred_element_type=F32))

    @pl.when(h == 0)
    def _():
        g_ref[...] = bif_ref[...] + g

    @pl.when(h != 0)
    def _():
        g_ref[...] += g

    ks = k * (A_HEAD_DIM ** -0.5)
    q_ref[...] = q.astype(BF16)
    k_ref[...] = ks.astype(BF16)
    v_ref[...] = v.astype(BF16)
    xc_ref[...] = xc.astype(BF16)
    if emit_kt:
        kt_ref[0] = ks.T.astype(BF16)


def _headwise_coefs(w):
    n = w.shape[0]
    rows = []
    for delta in range(-3, 4):
        cols = []
        for d in range(4):
            c = d + delta
            cols.append(w[:, c, d] if 0 <= c < 4 else jnp.zeros((n,), w.dtype))
        rows.append(jnp.stack(cols, axis=1).reshape(4 * n))
    rows.append(jnp.zeros((4 * n,), w.dtype))
    return jnp.stack(rows)


def mlstm_pre(up, halo, nseq, ntile, lsub, nsub, lw, zero_first, emit_kt):
    lt = nsub * lsub
    rows = nseq * ntile * lt
    hd = A_HEAD_DIM
    if zero_first:
        assert nsub == 1

        def prev_map(s, t, h):
            return (jnp.maximum((s * ntile + t) * (lt // 8) - 1, 0), h)
    else:
        assert ntile == 1

        def prev_map(s, t, h):
            return (s, h)
    wspec = pl.BlockSpec((8, hd), lambda s, t, h: (0, h))
    row_spec = pl.BlockSpec((lt, hd), lambda s, t, h: (s * ntile + t, h))
    out_specs = [row_spec, row_spec, row_spec, row_spec,
                 pl.BlockSpec((lt, 128), lambda s, t, h: (s * ntile + t, 0))]
    out_shape = [jax.ShapeDtypeStruct((rows, A_INNER), BF16)] * 4 + [jax.ShapeDtypeStruct((rows, 128), F32)]
    if emit_kt:
        out_specs.append(pl.BlockSpec((1, hd, lt), lambda s, t, h: (s * ntile + t, h, 0)))
        out_shape.append(jax.ShapeDtypeStruct((nseq * ntile, A_INNER, lt), BF16))
    return pl.pallas_call(
        functools.partial(_mlstm_pre_kernel, lsub=lsub, nsub=nsub, zero_first=zero_first, emit_kt=emit_kt),
        grid=(nseq, ntile, A_HEADS),
        in_specs=[row_spec,
                  pl.BlockSpec((8 * nsub, hd), prev_map),
                  wspec,
                  pl.BlockSpec((1, hd), lambda s, t, h: (0, h)),
                  wspec, wspec, wspec,
                  pl.BlockSpec((3, hd, 128), lambda s, t, h: (0, h, 0)),
                  pl.BlockSpec((1, 128), lambda s, t, h: (0, 0))],
        out_specs=out_specs,
        out_shape=out_shape,
        scratch_shapes=[pltpu.VMEM((nsub * (lsub + 8), hd), F32)],
        compiler_params=_params(("parallel", "parallel", "arbitrary"), 40),
        name="mlstm_pre",
    )(up, halo, lw["conv_w8"], lw["conv_b"], lw["cq"], lw["ck"], lw["cv"], lw["wif"], lw["bif"])


def _mlstm_cell_kernel(*refs, lc, valid, has_init, has_prev):
    (q_ref, k_ref, kt_ref, v_ref, xc_ref, z_ref, igc_ref, fgc_ref, igr_ref, fgr_ref, mh_ref, skip_ref) = refs[:12]
    rest = refs[12:]
    if has_init:
        c0_ref, n0_ref, m0_ref = rest[:3]
        rest = rest[3:]
    if has_prev:
        rest = rest[1:]
    o_ref, cout_ref, nout_ref, mout_ref, c_sc, n_sc, m_sc = rest
    c = pl.program_id(2)

    @pl.when(c == 0)
    def _():
        if has_init:
            c_sc[...] = c0_ref[0, 0, 0]
            n_sc[...] = n0_ref[0, 0, 0]
            m_sc[...] = m0_ref[0, 0, 0]
        else:
            c_sc[...] = jnp.zeros_like(c_sc)
            n_sc[...] = jnp.zeros_like(n_sc)
            m_sc[...] = jnp.full_like(m_sc, NEG)

    def logsig(x):
        return jnp.minimum(x, 0.0) - jnp.log(1.0 + jnp.exp(-jnp.abs(x)))

    ig_c = igc_ref[0, 0]
    ig_r = igr_ref[0, 0]
    lf_c = logsig(fgc_ref[0, 0])
    lf_r = logsig(fgr_ref[0, 0])
    row = lax.broadcasted_iota(jnp.int32, (lc, lc), 0)
    col = lax.broadcasted_iota(jnp.int32, (lc, lc), 1)
    if valid < lc:
        rid = lax.broadcasted_iota(jnp.int32, (lc, 1), 0)
        cid = lax.broadcasted_iota(jnp.int32, (1, lc), 1)
        ig_c = jnp.where(rid < valid, ig_c, NEG)
        ig_r = jnp.where(cid < valid, ig_r, NEG)
        lf_c = jnp.where(rid < valid, lf_c, 0.0)
        lf_r = jnp.where(cid < valid, lf_r, 0.0)
    causal = row >= col
    b_c = jnp.sum(jnp.where(causal, lf_r, 0.0), axis=1, keepdims=True)
    b_r = jnp.sum(jnp.where(row <= col, lf_c, 0.0), axis=0, keepdims=True)
    b_last = jnp.sum(lf_r, axis=1, keepdims=True)

    m_st = m_sc[...]
    d = jnp.where(causal, b_c - b_r + ig_r, NEG)
    m_inter = b_c + m_st
    m_t = jnp.maximum(m_inter, jnp.max(d, axis=1, keepdims=True))
    q = q_ref[...]
    s = lax.dot_general(q, k_ref[...], (((1,), (1,)), ((), ())), preferred_element_type=F32)
    s = s * jnp.exp(d - m_t)
    w_inter = jnp.exp(m_inter - m_t)
    c_st = c_sc[...]
    n_st = n_sc[...]
    v = v_ref[...]
    num = (jnp.dot(s.astype(BF16), v, preferred_element_type=F32)
           + w_inter * jnp.dot(q, c_st.astype(BF16), preferred_element_type=F32))
    qn = jnp.sum(q.astype(F32) * n_st, axis=1, keepdims=True)
    den = jnp.sum(s, axis=1, keepdims=True) + w_inter * qn
    hcell = num / jnp.maximum(jnp.abs(den), jnp.exp(-m_t))

    m_new = m_t[lc - 1:lc, :]
    w_r = jnp.exp(b_last - b_r + ig_r - m_new)
    w_c = jnp.exp(b_last - b_c + ig_c - m_new)
    decay = jnp.exp(b_last + m_st - m_new)
    kw_t = (kt_ref[0].astype(F32) * w_r).astype(BF16)
    c_new = decay * c_st + jnp.dot(kw_t, v, preferred_element_type=F32)
    n_new = decay * n_st + jnp.sum(k_ref[...].astype(F32) * w_c, axis=0, keepdims=True)
    c_sc[...] = c_new
    n_sc[...] = n_new
    m_sc[...] = m_new

    mu = jnp.mean(hcell, axis=1, keepdims=True)
    hc = hcell - mu
    var = jnp.mean(hc * hc, axis=1, keepdims=True)
    hn = hc * lax.rsqrt(var + EPS) * mh_ref[...]
    z = z_ref[...]
    o_ref[...] = ((hn + skip_ref[...] * xc_ref[...].astype(F32)) * (z * _sigmoid(z))).astype(BF16)

    @pl.when(c == pl.num_programs(2) - 1)
    def _():
        cout_ref[0, 0, 0] = c_new
        nout_ref[0, 0] = n_new
        mout_ref[0, 0] = jnp.broadcast_to(m_new, (1, 128))


def mlstm_cell(q, k, kt, v, xc, up, gates, lw, nseq, nchunk, lc, valid, layer, nlayer, init, c_prev):
    hd = A_HEAD_DIM
    nt = nseq * nchunk
    g = gates[:, :2 * A_HEADS].reshape(nt, lc, 2 * A_HEADS)
    g_col = jnp.transpose(g, (0, 2, 1))[..., None]
    g_row = jnp.transpose(g, (0, 2, 1))[:, :, None, :]
    row_spec = pl.BlockSpec((lc, hd), lambda s, h, c: (s * nchunk + c, h))
    in_specs = [row_spec, row_spec,
                pl.BlockSpec((1, hd, lc), lambda s, h, c: (s * nchunk + c, h, 0)),
                row_spec, row_spec,
                pl.BlockSpec((lc, hd), lambda s, h, c: (s * nchunk + c, A_HEADS + h)),
                pl.BlockSpec((1, 1, lc, 1), lambda s, h, c: (s * nchunk + c, h, 0, 0)),
                pl.BlockSpec((1, 1, lc, 1), lambda s, h, c: (s * nchunk + c, A_HEADS + h, 0, 0)),
                pl.BlockSpec((1, 1, 1, lc), lambda s, h, c: (s * nchunk + c, h, 0, 0)),
                pl.BlockSpec((1, 1, 1, lc), lambda s, h, c: (s * nchunk + c, A_HEADS + h, 0, 0)),
                pl.BlockSpec((1, hd), lambda s, h, c: (0, h)),
                pl.BlockSpec((1, hd), lambda s, h, c: (0, h))]
    args = [q, k, kt, v, xc, up, g_col, g_col, g_row, g_row, lw["mh_w"], lw["skip"]]
    has_init = init is not None
    if has_init:
        c0, n0, m0 = init
        in_specs += [pl.BlockSpec((1, 1, 1, hd, hd), lambda s, h, c: (layer, s, h, 0, 0)),
                     pl.BlockSpec((1, 1, 1, 1, hd), lambda s, h, c: (layer, s, h, 0, 0)),
                     pl.BlockSpec((1, 1, 1, 1, 1), lambda s, h, c: (layer, s, h, 0, 0))]
        args += [c0, n0.reshape(n0.shape[:3] + (1, hd)), m0.reshape(m0.shape + (1, 1))]
    has_prev = c_prev is not None
    aliases = {}
    if has_prev:
        aliases = {len(args): 1}
        in_specs.append(pl.BlockSpec(memory_space=pl.ANY))
        args.append(c_prev)
    out, c_out, n_out, m_out = pl.pallas_call(
        functools.partial(_mlstm_cell_kernel, lc=lc, valid=valid, has_init=has_init, has_prev=has_prev),
        grid=(nseq, A_HEADS, nchunk),
        in_specs=in_specs,
        out_specs=[row_spec,
                   pl.BlockSpec((1, 1, 1, hd, hd), lambda s, h, c: (layer, s, h, 0, 0)),
                   pl.BlockSpec((1, 1, 1, hd), lambda s, h, c: (s, h, 0, 0)),
                   pl.BlockSpec((1, 1, 1, 128), lambda s, h, c: (s, h, 0, 0))],
        out_shape=[jax.ShapeDtypeStruct((nt * lc, A_INNER), BF16),
                   jax.ShapeDtypeStruct((nlayer, nseq, A_HEADS, hd, hd), F32),
                   jax.ShapeDtypeStruct((nseq, A_HEADS, 1, hd), F32),
                   jax.ShapeDtypeStruct((nseq, A_HEADS, 1, 128), F32)],
        scratch_shapes=[pltpu.VMEM((hd, hd), F32), pltpu.VMEM((1, hd), F32), pltpu.VMEM((1, 1), F32)],
        input_output_aliases=aliases,
        compiler_params=_params(("parallel", "parallel", "arbitrary"), 40),
        name="mlstm_cell",
    )(*args)
    return out, c_out, n_out[:, :, 0, :], m_out[:, :, 0, 0]


def _mlstm_weights(layer, a_norm_pre, a_w_up, a_conv_w, a_conv_b, a_w_q, a_w_k, a_w_v, a_w_if, a_b_if,
                   a_mh_w, a_skip, a_w_down, a_norm_post):
    wif = a_w_if[layer].reshape(3, A_INNER, 2 * A_HEADS)
    wif = jnp.pad(wif, ((0, 0), (0, 0), (0, 128 - 2 * A_HEADS))).astype(BF16)
    return {
        "norm_pre": a_norm_pre[layer],
        "w_up": a_w_up[layer].astype(BF16),
        "conv_w8": jnp.pad(a_conv_w[layer], ((0, 8 - A_CONV), (0, 0))),
        "conv_b": a_conv_b[layer].reshape(1, A_INNER),
        "cq": _headwise_coefs(a_w_q[layer]),
        "ck": _headwise_coefs(a_w_k[layer]),
        "cv": _headwise_coefs(a_w_v[layer]),
        "wif": wif,
        "bif": jnp.pad(a_b_if[layer], (0, 128 - 2 * A_HEADS)).reshape(1, 128),
        "mh_w": a_mh_w[layer].reshape(1, A_INNER),
        "skip": a_skip[layer].reshape(1, A_INNER),
        "w_down": a_w_down[layer].astype(BF16),
        "norm_post": a_norm_post[layer],
    }


def mlstm_layer_prompt(x, lw, nseq, seq, layer, nlayer, c_prev):
    lc = CHUNK_P
    nchunk = seq // lc
    up = norm_matmul(x, lw["norm_pre"], lw["w_up"], 512, 1024)
    q, k, v, xc, gates, kt = mlstm_pre(up, up, nseq, nchunk, lc, 1, lw, zero_first=True, emit_kt=True)
    out, c, n, m = mlstm_cell(q, k, kt, v, xc, up, gates, lw, nseq, nchunk, lc, lc, layer, nlayer, None, c_prev)
    x_new = matmul_norm_res(out, lw["w_down"], lw["norm_post"], x, 512)
    conv_new = up.reshape(nseq, seq, 2 * A_INNER)[:, seq - (A_CONV - 1):, :A_INNER]
    return x_new, c, n, m, conv_new


def mlstm_layer_sample(x, lw, nseq, ds, layer, state, conv0, c_prev):
    nsub = 32
    assert nseq % nsub == 0
    up = norm_matmul(x, lw["norm_pre"], lw["w_up"], 512, 1024)
    halo = jnp.pad(conv0, ((0, 0), (8 - (A_CONV - 1), 0), (0, 0))).reshape(nseq * 8, A_INNER)
    q, k, v, xc, gates = mlstm_pre(up, halo, nseq // nsub, 1, SPAD, nsub, lw, zero_first=False, emit_kt=False)
    kt = jnp.transpose(k.reshape(nseq, SPAD, A_INNER), (0, 2, 1))
    out, c, n, m = mlstm_cell(q, k, kt, v, xc, up, gates, lw, nseq, 1, SPAD, ds, layer, state[0].shape[0],
                              state, c_prev)
    x_new = matmul_norm_res(out, lw["w_down"], lw["norm_post"], x, 512)
    xm = up.reshape(nseq, SPAD, 2 * A_INNER)[:, :ds, :A_INNER]
    conv_new = jnp.concatenate([conv0, xm], axis=1)[:, ds:]
    return x_new, c, n, m, conv_new


CMP_PACK = 4


def _compress_kernel(pt_ref, *refs, npages, feature_major):
    del pt_ref
    pages = refs[:npages]
    pos_ref, w1_ref, b1_ref, w2_ref, b2_ref, o_ref, stage_ref, acc_ref = refs[npages:]
    nblk = acc_ref.shape[1]
    ncb = stage_ref.shape[0]
    for p in range(npages):
        for cb in range(ncb):
            if feature_major:
                blk = pages[p][0, cb * 128:(cb + 1) * 128, :].T
            else:
                blk = pages[p][0, :, cb * 128:(cb + 1) * 128]
            stage_ref[cb, p * PAGE:(p + 1) * PAGE, :] = blk

    low = _low_half(nblk)
    for quad in range(CMP_STRIDE // CMP_PACK):
        for cb in range(ncb):
            xs = [stage_ref[cb, pl.ds(CMP_PACK * quad + i, nblk, stride=CMP_STRIDE), :] for i in range(CMP_PACK)]
            for gg in range(2):
                if gg == 0:
                    tiles = [jnp.where(low, xs[i], pltpu.roll(xs[i + 1], B_HEAD_DIM, axis=1))
                             for i in range(0, CMP_PACK, 2)]
                else:
                    tiles = [jnp.where(low, pltpu.roll(xs[i], B_HEAD_DIM, axis=1), xs[i + 1])
                             for i in range(0, CMP_PACK, 2)]
                packed = jnp.concatenate(tiles, axis=1)
                c = 2 * cb + gg
                slot = c // B_KV_HEADS
                for half in range(2):
                    idx = half * (CMP_STRIDE // CMP_PACK) + quad
                    xb = (packed + pos_ref[slot, idx]).astype(BF16)
                    d = jnp.dot(xb, w1_ref[slot, idx], preferred_element_type=F32)
                    if quad == 0:
                        acc_ref[2 * c + half] = d
                    else:
                        acc_ref[2 * c + half] += d

    out = jnp.zeros((nblk, 2 * B_KV_HEADS * B_HEAD_DIM), F32)
    for c in range(2 * B_KV_HEADS):
        slot = c // B_KV_HEADS
        nxt = pltpu.roll(acc_ref[2 * c + 1], nblk - 1, axis=0)
        hpre = acc_ref[2 * c] + nxt + b1_ref[slot]
        hid = hpre * _sigmoid(hpre)
        out = out + jnp.dot(hid.astype(BF16), w2_ref[c], preferred_element_type=F32)
    out = out + b2_ref[...]
    rid = lax.broadcasted_iota(jnp.int32, (nblk, 1), 0)
    o_ref[0] = jnp.where(rid < nblk - 1, out, 0.0)


def compress(pages_arr, table, nb, cw, feature_major):
    npages = table.shape[0] // nb
    nblk = npages * (PAGE // CMP_STRIDE)
    width = 2 * B_KV_HEADS * B_HEAD_DIM
    npk = CMP_LEN // CMP_PACK

    def page_map(b, pt, p):
        return (pt[b * npages + p], 0, 0)

    def const(shape):
        return pl.BlockSpec(shape, lambda b, pt: (0,) * len(shape))

    page_block = (1, width, PAGE) if feature_major else (1, PAGE, width)
    grid_spec = pltpu.PrefetchScalarGridSpec(
        num_scalar_prefetch=1,
        grid=(nb,),
        in_specs=[pl.BlockSpec(page_block, functools.partial(page_map, p=p)) for p in range(npages)]
        + [const((2, npk, 1, CMP_PACK * B_HEAD_DIM)), const((2, npk, CMP_PACK * B_HEAD_DIM, CMP_HIDDEN)),
           const((2, 1, CMP_HIDDEN)), const((2 * B_KV_HEADS, CMP_HIDDEN, width)), const((1, width))],
        out_specs=pl.BlockSpec((1, nblk, width), lambda b, pt: (b, 0, 0)),
        scratch_shapes=[pltpu.VMEM((width // 128, npages * PAGE, 128), F32),
                        pltpu.VMEM((4 * B_KV_HEADS, nblk, CMP_HIDDEN), F32)],
    )
    return pl.pallas_call(
        functools.partial(_compress_kernel, npages=npages, feature_major=feature_major),
        grid_spec=grid_spec,
        out_shape=jax.ShapeDtypeStruct((nb, nblk, width), F32),
        compiler_params=_params(("arbitrary",), 48),
        name="nsa_compress",
    )(table, *([pages_arr] * npages), cw["pos"], cw["w1"], cw["b1"], cw["w2p"], cw["b2p"])


def _compress_weights(cmp_pos, cmp_w1, cmp_b1, cmp_w2, cmp_b2):
    width = 2 * B_KV_HEADS * B_HEAD_DIM
    npk = CMP_LEN // CMP_PACK
    w2p = jnp.zeros((2 * B_KV_HEADS, CMP_HIDDEN, width), F32)
    for c in range(2 * B_KV_HEADS):
        w2p = w2p.at[c, :, c * B_HEAD_DIM:(c + 1) * B_HEAD_DIM].set(cmp_w2[c // B_KV_HEADS])
    return {
        "pos": cmp_pos.reshape(2, npk, 1, CMP_PACK * B_HEAD_DIM),
        "w1": cmp_w1.reshape(2, npk, CMP_PACK * B_HEAD_DIM, CMP_HIDDEN).astype(BF16),
        "b1": cmp_b1.reshape(2, 1, CMP_HIDDEN),
        "w2p": w2p.astype(BF16),
        "b2p": jnp.tile(cmp_b2[:, None, :], (1, B_KV_HEADS, 1)).reshape(1, width),
    }


def _softmax_masked(s, mask):
    s = jnp.where(mask, s, NEG)
    e = jnp.exp(s - jnp.max(s, axis=1, keepdims=True))
    return jnp.where(mask, e / jnp.sum(e, axis=1, keepdims=True), 0.0)


def _dot_nt(a, b):
    return lax.dot_general(a, b, (((1,), (1,)), ((), ())), preferred_element_type=F32)


def _low_half(tq):
    return lax.broadcasted_iota(jnp.int32, (tq, 128), 1) < B_HEAD_DIM


def _pad_heads(qblk, gl, tq):
    low = _low_half(tq)
    keep = low if gl == 0 else jnp.logical_not(low)
    tiles = []
    for i in range(4):
        hh = gl * 4 + i
        tile = qblk[:, (hh // 2) * 128:(hh // 2 + 1) * 128]
        if hh % 2 != gl:
            tile = pltpu.roll(tile, B_HEAD_DIM, axis=1)
        tiles.append(jnp.where(keep, tile, 0.0))
    return jnp.concatenate(tiles, axis=0).astype(BF16)


def _select_blocks(psum, qpos1, cover, ns):
    hi = psum.astype(BF16)
    lo = (psum - hi.astype(F32)).astype(BF16)
    imp = jnp.dot(hi, cover, preferred_element_type=F32) + jnp.dot(lo, cover, preferred_element_type=F32)
    s_idx = lax.broadcasted_iota(jnp.int32, (1, 128), 1)
    cur = jnp.right_shift(qpos1, 6)
    assert SEL_LEN == 64
    valid = s_idx * SEL_LEN <= qpos1
    forced = (s_idx == 0) | (s_idx == cur) | (s_idx == cur - 1)
    score = jnp.where(valid, imp + jnp.where(forced, FORCE_BONUS, 0.0), NEG)
    score = jnp.where(s_idx < ns, score, 2 * NEG)
    rank = jnp.zeros_like(score)
    for sp in range(ns):
        colv = score[:, sp:sp + 1]
        ahead = (colv > score) | ((colv == score) & (s_idx > sp))
        rank = rank + jnp.where(ahead, 1.0, 0.0)
    return jnp.where((rank < min(SEL_TOPN, ns)) & (s_idx < ns), 1.0, 0.0)


def _select_blocks_t(psum, qpos_row, cover_t, ns):
    tq = psum.shape[0]
    pt = psum.T
    hi = pt.astype(BF16)
    lo = (pt - hi.astype(F32)).astype(BF16)
    imp = jnp.dot(cover_t, hi, preferred_element_type=F32) + jnp.dot(cover_t, lo, preferred_element_type=F32)
    nr = -(-ns // 8) * 8
    imp = imp[0:nr]
    s_idx = lax.broadcasted_iota(jnp.int32, (nr, 1), 0)
    cur = jnp.right_shift(qpos_row, 6)
    valid = s_idx * SEL_LEN <= qpos_row
    forced = (s_idx == 0) | (s_idx == cur) | (s_idx == cur - 1)
    score = jnp.where(valid, imp + jnp.where(forced, FORCE_BONUS, 0.0), NEG)
    score = jnp.where(s_idx < ns, score, 2 * NEG)
    rank = jnp.zeros_like(score)
    for sp in range(ns):
        rowv = score[sp:sp + 1, :]
        ahead = (rowv > score) | ((rowv == score) & (s_idx > sp))
        rank = rank + jnp.where(ahead, 1.0, 0.0)
    sel_t = jnp.where((rank < min(SEL_TOPN, ns)) & (s_idx < ns), 1.0, 0.0)
    return jnp.concatenate([sel_t, jnp.zeros((128 - nr, tq), F32)], axis=0).T


def _cmp_branch(qs, tq, qpos, kc, vc):
    n_idx = lax.broadcasted_iota(jnp.int32, (1, 128), 1)
    p_c = _softmax_masked(_dot_nt(qs, kc), n_idx * CMP_STRIDE + (CMP_LEN - 1) <= qpos)
    o_c = jnp.dot(p_c.astype(BF16), vc, preferred_element_type=F32)
    return o_c, p_c[0:tq] + p_c[tq:2 * tq] + p_c[2 * tq:3 * tq] + p_c[3 * tq:4 * tq]


def _softmax_pv(s, tq, bias, pv):
    nk = s.shape[1]
    s3 = s.reshape(4, tq, nk) + bias[None]
    e = jnp.exp(s3 - jnp.max(s3, axis=2, keepdims=True))
    den = jnp.sum(e, axis=2, keepdims=True)
    o = pv(e.reshape(4 * tq, nk).astype(BF16))
    return o * (1.0 / den.reshape(4 * tq, 1))


def _combine(o_heads, sg, zs, tq):
    low = _low_half(tq)
    y = jnp.zeros((tq, 512), F32)
    for br in range(3):
        tiles = []
        for j in range(4):
            c = br * 8 + 2 * j
            a = o_heads[br][2 * j] * sg[:, c:c + 1]
            b = o_heads[br][2 * j + 1] * sg[:, c + 1:c + 2]
            if j // 2 == 0:
                tiles.append(jnp.where(low, a, pltpu.roll(b, B_HEAD_DIM, axis=1)))
            else:
                tiles.append(jnp.where(low, pltpu.roll(a, B_HEAD_DIM, axis=1), b))
        z = zs[br]
        y = y + jnp.concatenate(tiles, axis=1) * (z * _sigmoid(z))
    return y


def _nsa_prompt_kernel(q_ref, z0_ref, z1_ref, z2_ref, gate_ref, kc_ref, vc_ref, ks_ref, vs_ref, kw_ref, vw_ref,
                       cover_t_ref, expand_ref, y_ref, osel_sc, *, tq, seq, kstep):
    q0 = pl.program_id(2) * tq
    qblk = q_ref[...] * (B_HEAD_DIM ** -0.5)
    qpos1 = q0 + lax.broadcasted_iota(jnp.int32, (tq, 1), 0)
    qpos_row = q0 + lax.broadcasted_iota(jnp.int32, (1, tq), 1)
    qpos = jnp.concatenate([qpos1] * 4, axis=0)
    wlen = WINDOW + tq
    wstart = pl.multiple_of(jnp.maximum(q0 - WINDOW, 0), 128)
    kw = kw_ref[pl.ds(wstart, wlen), :]
    vw = vw_ref[pl.ds(wstart, wlen), :]
    kpos_w = wstart + lax.broadcasted_iota(jnp.int32, (1, wlen), 1)
    wbias = jnp.where((kpos_w <= qpos1) & (kpos_w > qpos1 - WINDOW), 0.0, NEG)
    kc = kc_ref[0].astype(BF16)
    vc = vc_ref[0].astype(BF16)
    nkeys = (q0 + tq + kstep - 1) // kstep
    o_heads = [[None] * 8 for _ in range(3)]
    for gl in range(2):
        qs = _pad_heads(qblk, gl, tq)
        o_c, psum = _cmp_branch(qs, tq, qpos, kc, vc)
        sel = _select_blocks_t(psum, qpos_row, cover_t_ref[...], seq // SEL_LEN).astype(BF16)
        for v in range(1, seq // kstep + 1):
            @pl.when(nkeys == v)
            def _(nk=v * kstep, gl=gl, qs=qs, sel=sel):
                selk = jnp.dot(sel, expand_ref[:, 0:nk], preferred_element_type=F32) > 0.5
                kpos_s = lax.broadcasted_iota(jnp.int32, (1, nk), 1)
                sbias = jnp.where(selk & (kpos_s <= qpos1), 0.0, NEG)
                s = _dot_nt(qs, ks_ref[0:nk, :])
                osel_sc[gl] = _softmax_pv(
                    s, tq, sbias, lambda e: jnp.dot(e, vs_ref[0:nk, :], preferred_element_type=F32))
        o_w = _softmax_pv(_dot_nt(qs, kw), tq, wbias, lambda e: jnp.dot(e, vw, preferred_element_type=F32))
        outs = (o_c, osel_sc[gl], o_w)
        for br in range(3):
            for i in range(4):
                o_heads[br][gl * 4 + i] = outs[br][i * tq:(i + 1) * tq]
    y = _combine(o_heads, _sigmoid(gate_ref[...]), (z0_ref[...], z1_ref[...], z2_ref[...]), tq)
    y_ref[...] = y.astype(BF16)


def _cover_matrix(transposed=False):
    n = np.arange(128)[:, None]
    s = np.arange(128)[None, :]
    cov = (n * CMP_STRIDE < s * SEL_LEN + SEL_LEN) & (n * CMP_STRIDE + CMP_LEN - 1 >= s * SEL_LEN)
    return jnp.asarray(cov.T if transposed else cov, BF16)


def _expand_matrix(nkeys):
    s = np.arange(128)[:, None]
    key = np.arange(nkeys)[None, :]
    return jnp.asarray(key // SEL_LEN == s, BF16)


def nsa_attend_prompt(pm, pg, cmp, kvb, nb, seq):
    tq = 128
    nq = seq // tq

    def rows(width, col):
        return pl.BlockSpec((tq, width), lambda b, gp, t: (b * nq + t, col(gp)))

    def seq_block(first):
        return pl.BlockSpec((seq, 128), lambda b, gp, t: (b, first + gp))

    kstep = 512
    assert seq % kstep == 0 and tq % 128 == 0
    return pl.pallas_call(
        functools.partial(_nsa_prompt_kernel, tq=tq, seq=seq, kstep=kstep),
        grid=(nb, 2, nq),
        in_specs=[rows(512, lambda gp: gp), rows(512, lambda gp: 2 + gp), rows(512, lambda gp: 4 + gp),
                  rows(512, lambda gp: 6 + gp), rows(128, lambda gp: gp),
                  pl.BlockSpec((1, 128, 128), lambda b, gp, t: (b, 0, gp)),
                  pl.BlockSpec((1, 128, 128), lambda b, gp, t: (b, 0, 2 + gp)),
                  seq_block(4), seq_block(6), seq_block(8), seq_block(10),
                  pl.BlockSpec((128, 128), lambda b, gp, t: (0, 0)),
                  pl.BlockSpec((128, seq), lambda b, gp, t: (0, 0))],
        out_specs=rows(512, lambda gp: gp),
        out_shape=jax.ShapeDtypeStruct((nb * seq, B_ATT), BF16),
        scratch_shapes=[pltpu.VMEM((2, 4 * tq, 128), F32)],
        compiler_params=_params(("parallel", "parallel", "arbitrary"), 56),
        name="nsa_attend_prompt",
    )(pm, pm, pm, pm, pg, cmp, cmp, kvb, kvb, kvb, kvb, _cover_matrix(transposed=True), _expand_matrix(seq))


def _nsa_sample_kernel(pt_ref, *refs, npages, past):
    del pt_ref
    pages = refs[:npages]
    (q_ref, z0_ref, z1_ref, z2_ref, gate_ref, cmp_ref, new_ref, win_ref, cover_ref, expand_ref, y_ref,
     kst_sc, vst_sc) = refs[npages:]
    tq = SPAD
    wrows = win_ref.shape[2]
    qpos1 = past + lax.broadcasted_iota(jnp.int32, (tq, 1), 0)
    qpos = jnp.concatenate([qpos1] * 4, axis=0)
    kpos_s = lax.broadcasted_iota(jnp.int32, (1, past + NEW_PAD), 1)
    kpos_w = (past - wrows) + lax.broadcasted_iota(jnp.int32, (1, wrows + NEW_PAD), 1)
    wbias = jnp.where((kpos_w <= qpos1) & (kpos_w > qpos1 - WINDOW) & (kpos_w >= 0), 0.0, NEG)
    sg = _sigmoid(gate_ref[...])
    ns = past // SEL_LEN + 1

    def new_rows(col):
        blk = new_ref[:, col:col + 128].astype(BF16)
        return jnp.concatenate([blk, jnp.zeros((NEW_PAD - tq, 128), BF16)], axis=0)

    for gp in range(2):
        lo, hi = gp * 128, (gp + 1) * 128
        for p in range(npages):
            kst_sc[gp, :, p * PAGE:(p + 1) * PAGE] = pages[p][0, lo:hi, :].astype(BF16)
            vst_sc[gp, :, p * PAGE:(p + 1) * PAGE] = pages[p][0, 256 + lo:256 + hi, :].astype(BF16)
        kwt = win_ref[0, lo:hi, :].astype(BF16)
        vwt = win_ref[0, 256 + lo:256 + hi, :].astype(BF16)
        ks_new, vs_new = new_rows(512 + lo), new_rows(768 + lo)
        kw_new, vw_new = new_rows(1024 + lo), new_rows(1280 + lo)
        kc = cmp_ref[0, :, lo:hi].astype(BF16)
        vc = cmp_ref[0, :, 256 + lo:256 + hi].astype(BF16)
        qblk = q_ref[:, gp * 512:(gp + 1) * 512] * (B_HEAD_DIM ** -0.5)
        o_heads = [[None] * 8 for _ in range(3)]
        for gl in range(2):
            qs = _pad_heads(qblk, gl, tq)
            o_c, psum = _cmp_branch(qs, tq, qpos, kc, vc)
            sel = _select_blocks(psum, qpos1, cover_ref[...], ns).astype(BF16)
            selk = jnp.dot(sel, expand_ref[...], preferred_element_type=F32) > 0.5
            sbias = jnp.where(selk & (kpos_s <= qpos1), 0.0, NEG)
            s = jnp.concatenate([jnp.dot(qs, kst_sc[gp], preferred_element_type=F32), _dot_nt(qs, ks_new)], axis=1)
            o_s = _softmax_pv(s, tq, sbias, lambda e: (
                _dot_nt(e[:, 0:past], vst_sc[gp]) + jnp.dot(e[:, past:], vs_new, preferred_element_type=F32)))
            s = jnp.concatenate([jnp.dot(qs, kwt, preferred_element_type=F32), _dot_nt(qs, kw_new)], axis=1)
            o_w = _softmax_pv(s, tq, wbias, lambda e: (
                _dot_nt(e[:, 0:wrows], vwt) + jnp.dot(e[:, wrows:], vw_new, preferred_element_type=F32)))
            outs = (o_c, o_s, o_w)
            for br in range(3):
                for i in range(4):
                    o_heads[br][gl * 4 + i] = outs[br][i * tq:(i + 1) * tq]
        zs = (z0_ref[:, gp * 512:(gp + 1) * 512], z1_ref[:, gp * 512:(gp + 1) * 512],
              z2_ref[:, gp * 512:(gp + 1) * 512])
        y_ref[:, gp * 512:(gp + 1) * 512] = _combine(o_heads, sg[:, lo:hi], zs, tq).astype(BF16)


def nsa_attend_sample(pm, pg, cmp, kv_new, cache_pages, table, cache_win, nb, past):
    npages = past // PAGE
    wrows = cache_win.shape[2]
    nks = past + NEW_PAD
    assert past % SEL_LEN == 0 and SPAD <= SEL_LEN

    def page_map(b, pt, p):
        return (pt[b * npages + p], 1, 0)

    def rows(width, col):
        return pl.BlockSpec((SPAD, width), lambda b, pt: (b, col))

    grid_spec = pltpu.PrefetchScalarGridSpec(
        num_scalar_prefetch=1,
        grid=(nb,),
        in_specs=[pl.BlockSpec((1, 512, PAGE), functools.partial(page_map, p=p)) for p in range(npages)]
        + [rows(B_ATT, 0), rows(B_ATT, 1), rows(B_ATT, 2), rows(B_ATT, 3), rows(256, 0),
           pl.BlockSpec((1, 128, 512), lambda b, pt: (b, 0, 0)),
           rows(1536, 0),
           pl.BlockSpec((1, 512, wrows), lambda b, pt: (b, 0, 0)),
           pl.BlockSpec((128, 128), lambda b, pt: (0, 0)),
           pl.BlockSpec((128, nks), lambda b, pt: (0, 0))],
        out_specs=rows(B_ATT, 0),
        scratch_shapes=[pltpu.VMEM((2, 128, past), BF16), pltpu.VMEM((2, 128, past), BF16)],
    )
    return pl.pallas_call(
        functools.partial(_nsa_sample_kernel, npages=npages, past=past),
        grid_spec=grid_spec,
        out_shape=jax.ShapeDtypeStruct((nb * SPAD, B_ATT), BF16),
        compiler_params=_params(("arbitrary",), 48),
        name="nsa_attend_sample",
    )(table, *([cache_pages] * npages), pm, pm, pm, pm, pg, cmp, kv_new, cache_win,
      _cover_matrix(), _expand_matrix(nks))


def _nsa_weights(lb, b_norm_pre, b_w_in, b_w_out, b_norm_post):
    w_in = b_w_in[lb]
    wg = w_in[:, 4 * B_ATT:].reshape(D_MODEL, 3, 2, 8)
    wg = jnp.transpose(wg, (0, 2, 1, 3)).reshape(D_MODEL, 2, 24)
    wg = jnp.pad(wg, ((0, 0), (0, 0), (0, 128 - 24))).reshape(D_MODEL, 256)
    return {"norm_pre": b_norm_pre[lb], "w_main": w_in[:, :4 * B_ATT].astype(BF16), "w_gate": wg.astype(BF16),
            "w_out": b_w_out[lb].astype(BF16), "norm_post": b_norm_post[lb]}


def kernel(x_prompt, x_sample, cache_kv, page_table, cache_win, state_C, state_n, state_m, state_conv,
           a_norm_pre, a_w_up, a_conv_w, a_conv_b, a_w_q, a_w_k, a_w_v, a_w_if, a_b_if,
           a_mh_w, a_skip, a_w_down, a_norm_post,
           kv_norm, w_kv, cmp_pos, cmp_w1, cmp_b1, cmp_w2, cmp_b2,
           b_norm_pre, b_w_in, b_w_out, b_norm_post):
    bp, sp = x_prompt.shape[0], x_prompt.shape[1]
    bs, ds = x_sample.shape[0], x_sample.shape[1]
    n_a = state_C.shape[0]
    n_b = b_w_in.shape[0]
    past = page_table.shape[1] * cache_kv.shape[1]
    assert cache_kv.shape[1] == PAGE and sp % PAGE == 0 and ds <= SPAD
    assert (past + ds - CMP_LEN) // CMP_STRIDE + 1 == past // CMP_STRIDE - 1
    xp = x_prompt.reshape(bp * sp, D_MODEL)
    xs = jnp.pad(x_sample, ((0, 0), (0, SPAD - ds), (0, 0))).reshape(bs * SPAD, D_MODEL)

    np_l, ns_l, mp_l, ms_l, vp_l, vs_l = [], [], [], [], [], []
    c_prompt = c_sample = None
    for layer in range(n_a):
        lw = _mlstm_weights(layer, a_norm_pre, a_w_up, a_conv_w, a_conv_b, a_w_q, a_w_k, a_w_v, a_w_if, a_b_if,
                            a_mh_w, a_skip, a_w_down, a_norm_post)
        xp, c_prompt, n1, m1, v1 = mlstm_layer_prompt(xp, lw, bp, sp, layer, n_a, c_prompt)
        xs, c_sample, n2, m2, v2 = mlstm_layer_sample(xs, lw, bs, ds, layer, (state_C, state_n, state_m),
                                                      state_conv[layer], c_sample)
        np_l.append(n1); mp_l.append(m1); vp_l.append(v1)
        ns_l.append(n2); ms_l.append(m2); vs_l.append(v2)

    nkv = w_kv.shape[1]
    w_kv_b = w_kv.astype(BF16)
    kvp, kvp_b = norm_matmul(xp, kv_norm, w_kv_b, 512, 512, also_bf16=True)
    kvs = norm_matmul(xs, kv_norm, w_kv_b, 512, 512)
    cw = _compress_weights(cmp_pos, cmp_w1, cmp_b1, cmp_w2, cmp_b2)
    table_p = jnp.arange(bp * (sp // PAGE), dtype=jnp.int32)
    table_s = page_table.reshape(-1).astype(jnp.int32)
    cache_pages = jnp.transpose(cache_kv, (0, 2, 3, 4, 1)).reshape(cache_kv.shape[0], -1, PAGE)
    cmp_p = compress(kvp.reshape(bp * sp // PAGE, PAGE, nkv), table_p, bp, cw, feature_major=False)
    cmp_s = compress(cache_pages, table_s, bs, cw, feature_major=True)
    wb = cache_win.shape[1]
    win2 = jnp.transpose(cache_win, (0, 2, 3, 4, 1)).reshape(bs, -1, wb)

    for lb in range(n_b):
        bw = _nsa_weights(lb, b_norm_pre, b_w_in, b_w_out, b_norm_post)
        pm = norm_matmul(xp, bw["norm_pre"], bw["w_main"], 512, 1024)
        pg = norm_matmul(xp, bw["norm_pre"], bw["w_gate"], 512, 256)
        yp = nsa_attend_prompt(pm, pg, cmp_p, kvp_b, bp, sp)
        xp = matmul_norm_res(yp, bw["w_out"], bw["norm_post"], xp, 512)
        pm = norm_matmul(xs, bw["norm_pre"], bw["w_main"], 512, 1024)
        pg = norm_matmul(xs, bw["norm_pre"], bw["w_gate"], 512, 256)
        ys = nsa_attend_sample(pm, pg, cmp_s, kvs, cache_pages, table_s, win2, bs, past)
        xs = matmul_norm_res(ys, bw["w_out"], bw["norm_post"], xs, 512)

    y_prompt = xp.reshape(bp, sp, D_MODEL)
    y_sample = xs.reshape(bs, SPAD, D_MODEL)[:, :ds]
    kvp3 = kvp.reshape(bp, sp, nkv)
    kvs3 = kvs.reshape(bs, SPAD, nkv)[:, :ds]
    ncache = 4 * B_KV_HEADS * B_HEAD_DIM
    kv_rows_prompt = kvp3[:, :, :ncache].reshape(bp, sp, 4, B_KV_HEADS, B_HEAD_DIM)
    kv_rows_sample = kvs3[:, :, :ncache].reshape(bs, ds, 4, B_KV_HEADS, B_HEAD_DIM)
    win_prompt = kvp3[:, sp - min(WINDOW, sp):, ncache:].reshape(bp, min(WINDOW, sp), 2, B_KV_HEADS, B_HEAD_DIM)
    win_new = kvs3[:, :, ncache:].reshape(bs, ds, 2, B_KV_HEADS, B_HEAD_DIM).astype(cache_win.dtype)
    win_sample = jnp.concatenate([cache_win, win_new], axis=1)[:, ds:]
    return (y_prompt, y_sample, kv_rows_prompt, kv_rows_sample, win_prompt, win_sample,
            c_prompt, c_sample, jnp.stack(np_l), jnp.stack(ns_l),
            jnp.stack(mp_l), jnp.stack(ms_l), jnp.stack(vp_l), jnp.stack(vs_l))
```

```python
import functools

import jax
import jax.numpy as jnp
import numpy as np
from jax import lax
from jax.experimental import pallas as pl
from jax.experimental.pallas import tpu as pltpu

F32 = jnp.float32
BF16 = jnp.bfloat16

D_MODEL = 1024
A_INNER = 2048
A_HEADS = 4
A_HEAD_DIM = 512
A_CONV = 4
B_HEADS = 16
B_HEAD_DIM = 64
B_KV_HEADS = 4
B_ATT = 1024
CMP_LEN = 32
CMP_STRIDE = 16
CMP_HIDDEN = 256
SEL_LEN = 64
SEL_TOPN = 16
WINDOW = 512
PAGE = 128
FORCE_BONUS = 1.0e3
NEG = -1.0e30
EPS = 1e-6

SPAD = 16
NEW_PAD = 128
CHUNK_P = 256
V7X_VMEM_BYTES = 64 * 2**20


def _params(sem, vmem_mb):
    assert vmem_mb * 2**20 < V7X_VMEM_BYTES
    return pltpu.CompilerParams(dimension_semantics=sem, vmem_limit_bytes=vmem_mb * 2**20)


def _sigmoid(x):
    return 1.0 / (1.0 + jnp.exp(-x))


def _norm_matmul_kernel(x_ref, nw_ref, w_ref, *rest):
    o_refs, xn_ref = rest[:-1], rest[-1]

    @pl.when(pl.program_id(1) == 0)
    def _():
        x = x_ref[...]
        ms = jnp.mean(x * x, axis=-1, keepdims=True)
        xn_ref[...] = (x * lax.rsqrt(ms + EPS) * nw_ref[...]).astype(BF16)

    y = jnp.dot(xn_ref[...], w_ref[...], preferred_element_type=F32)
    for o_ref in o_refs:
        o_ref[...] = y.astype(o_ref.dtype)


def norm_matmul(x, nw, w, tm, tn, also_bf16=False):
    m, k = x.shape
    n = w.shape[1]
    assert m % tm == 0 and n % tn == 0
    o_spec = pl.BlockSpec((tm, tn), lambda i, j: (i, j))
    out_specs = [o_spec]
    out_shape = [jax.ShapeDtypeStruct((m, n), F32)]
    if also_bf16:
        out_specs.append(o_spec)
        out_shape.append(jax.ShapeDtypeStruct((m, n), BF16))
    res = pl.pallas_call(
        _norm_matmul_kernel,
        grid=(m // tm, n // tn),
        in_specs=[pl.BlockSpec((tm, k), lambda i, j: (i, 0)),
                  pl.BlockSpec((1, k), lambda i, j: (0, 0)),
                  pl.BlockSpec((k, tn), lambda i, j: (0, j))],
        out_specs=out_specs,
        out_shape=out_shape,
        scratch_shapes=[pltpu.VMEM((tm, k), BF16)],
        compiler_params=_params(("parallel", "arbitrary"), 40),
        name="norm_matmul",
    )(x, nw.reshape(1, k), w)
    return res if also_bf16 else res[0]


def _matmul_norm_res_kernel(a_ref, w_ref, nw_ref, x_ref, o_ref):
    y = jnp.dot(a_ref[...], w_ref[...], preferred_element_type=F32)
    ms = jnp.mean(y * y, axis=-1, keepdims=True)
    o_ref[...] = x_ref[...] + y * lax.rsqrt(ms + EPS) * nw_ref[...]


def matmul_norm_res(a, w, nw, x, tm):
    m, k = a.shape
    n = w.shape[1]
    assert m % tm == 0
    return pl.pallas_call(
        _matmul_norm_res_kernel,
        grid=(m // tm,),
        in_specs=[pl.BlockSpec((tm, k), lambda i: (i, 0)),
                  pl.BlockSpec((k, n), lambda i: (0, 0)),
                  pl.BlockSpec((1, n), lambda i: (0, 0)),
                  pl.BlockSpec((tm, n), lambda i: (i, 0))],
        out_specs=pl.BlockSpec((tm, n), lambda i: (i, 0)),
        out_shape=jax.ShapeDtypeStruct((m, n), F32),
        compiler_params=_params(("parallel",), 40),
        name="matmul_norm_res",
    )(a, w, nw.reshape(1, n), x)


def _mlstm_pre_kernel(xm_ref, prev_ref, cw_ref, cb_ref, bd_ref, wif_ref, bif_ref,
                      q_ref, k_ref, v_ref, xc_ref, g_ref, *rest, lsub, nsub, zero_first, emit_kt):
    if emit_kt:
        kt_ref, cat_ref = rest
    else:
        (cat_ref,) = rest
    t = pl.program_id(1)
    h = pl.program_id(2)
    x = xm_ref[...]
    prev = prev_ref[...]
    if zero_first:
        prev = jnp.where(t == 0, 0.0, prev)
    cw = cw_ref[...]
    pieces = []
    for s in range(nsub):
        base = s * (lsub + 8)
        xs = x[s * lsub:(s + 1) * lsub]
        cat_ref[base:base + 8, :] = prev[s * 8:(s + 1) * 8]
        cat_ref[base + 8:base + 8 + lsub, :] = xs
        acc = cb_ref[...] + xs * cw[A_CONV - 1:A_CONV]
        for j in range(1, A_CONV):
            acc = acc + cat_ref[base + 8 - j:base + 8 - j + lsub, :] * cw[A_CONV - 1 - j:A_CONV - j]
        pieces.append(acc)
    xconv = pieces[0] if nsub == 1 else jnp.concatenate(pieces, axis=0)
    xc = xconv * _sigmoid(xconv)

    xcb = xc.astype(BF16)
    xb = x.astype(BF16)
    ntile = x.shape[1] // 128

    def headwise(src, which):
        return jnp.concatenate(
            [jnp.dot(src[:, j * 128:(j + 1) * 128], bd_ref[which, j], preferred_element_type=F32)
             for j in range(ntile)], axis=1)

    q = headwise(xcb, 0)
    k = headwise(xcb, 1)
    v = headwise(xb, 2)

    g = (jnp.dot(q.astype(BF16), wif_ref[0], preferred_element_type=F32)
         + jnp.dot(k.astype(BF16), wif_ref[1], preferred_element_type=F32)
         + jnp.dot(v.astype(BF16), wif_ref[2], preferred_element_type=F32))

    @pl.when(h == 0)
    def _():
        g_ref[...] = bif_ref[...] + g

    @pl.when(h != 0)
    def _():
        g_ref[...] += g

    ks = k * (A_HEAD_DIM ** -0.5)
    q_ref[...] = q.astype(BF16)
    k_ref[...] = ks.astype(BF16)
    v_ref[...] = v.astype(BF16)
    xc_ref[...] = xc.astype(BF16)
    if emit_kt:
        kt_ref[0] = ks.T.astype(BF16)


def _headwise_tiles(w):
    n, c, _ = w.shape
    per = 128 // c
    wt = w.reshape(n // per, per, c, c)
    eye = jnp.eye(per, dtype=w.dtype)
    return jnp.einsum("nm,tncd->tncmd", eye, wt).reshape(n // per, 128, 128)


def mlstm_pre(up, halo, nseq, ntile, lsub, nsub, lw, zero_first, emit_kt):
    lt = nsub * lsub
    rows = nseq * ntile * lt
    hd = A_HEAD_DIM
    if zero_first:
        assert nsub == 1

        def prev_map(s, t, h):
            return (jnp.maximum((s * ntile + t) * (lt // 8) - 1, 0), h)
    else:
        assert ntile == 1

        def prev_map(s, t, h):
            return (s, h)
    wspec = pl.BlockSpec((8, hd), lambda s, t, h: (0, h))
    row_spec = pl.BlockSpec((lt, hd), lambda s, t, h: (s * ntile + t, h))
    out_specs = [row_spec, row_spec, row_spec, row_spec,
                 pl.BlockSpec((lt, 128), lambda s, t, h: (s * ntile + t, 0))]
    out_shape = [jax.ShapeDtypeStruct((rows, A_INNER), BF16)] * 4 + [jax.ShapeDtypeStruct((rows, 128), F32)]
    if emit_kt:
        out_specs.append(pl.BlockSpec((1, hd, lt), lambda s, t, h: (s * ntile + t, h, 0)))
        out_shape.append(jax.ShapeDtypeStruct((nseq * ntile, A_INNER, lt), BF16))
    return pl.pallas_call(
        functools.partial(_mlstm_pre_kernel, lsub=lsub, nsub=nsub, zero_first=zero_first, emit_kt=emit_kt),
        grid=(nseq, ntile, A_HEADS),
        in_specs=[row_spec,
                  pl.BlockSpec((8 * nsub, hd), prev_map),
                  wspec,
                  pl.BlockSpec((1, hd), lambda s, t, h: (0, h)),
                  pl.BlockSpec((3, hd // 128, 128, 128), lambda s, t, h: (0, h, 0, 0)),
                  pl.BlockSpec((3, hd, 128), lambda s, t, h: (0, h, 0)),
                  pl.BlockSpec((1, 128), lambda s, t, h: (0, 0))],
        out_specs=out_specs,
        out_shape=out_shape,
        scratch_shapes=[pltpu.VMEM((nsub * (lsub + 8), hd), F32)],
        compiler_params=_params(("parallel", "parallel", "arbitrary"), 40),
        name="mlstm_pre",
    )(up, halo, lw["conv_w8"], lw["conv_b"], lw["bd"], lw["wif"], lw["bif"])


def _mlstm_cell_kernel(*refs, lc, valid, has_init, has_prev):
    (q_ref, k_ref, kt_ref, v_ref, xc_ref, z_ref, igc_ref, fgc_ref, igr_ref, fgr_ref, mh_ref, skip_ref) = refs[:12]
    rest = refs[12:]
    if has_init:
        c0_ref, n0_ref, m0_ref = rest[:3]
        rest = rest[3:]
    if has_prev:
        rest = rest[1:]
    o_ref, cout_ref, nout_ref, mout_ref, c_sc, n_sc, m_sc = rest
    c = pl.program_id(2)

    @pl.when(c == 0)
    def _():
        if has_init:
            c_sc[...] = c0_ref[0, 0, 0]
            n_sc[...] = n0_ref[0, 0, 0]
            m_sc[...] = m0_ref[0, 0, 0]
        else:
            c_sc[...] = jnp.zeros_like(c_sc)
            n_sc[...] = jnp.zeros_like(n_sc)
            m_sc[...] = jnp.full_like(m_sc, NEG)

    def logsig(x):
        return jnp.minimum(x, 0.0) - jnp.log(1.0 + jnp.exp(-jnp.abs(x)))

    ig_c = igc_ref[0, 0]
    ig_r = igr_ref[0, 0]
    lf_c = logsig(fgc_ref[0, 0])
    lf_r = logsig(fgr_ref[0, 0])
    row = lax.broadcasted_iota(jnp.int32, (lc, lc), 0)
    col = lax.broadcasted_iota(jnp.int32, (lc, lc), 1)
    if valid < lc:
        rid = lax.broadcasted_iota(jnp.int32, (lc, 1), 0)
        cid = lax.broadcasted_iota(jnp.int32, (1, lc), 1)
        ig_c = jnp.where(rid < valid, ig_c, NEG)
        ig_r = jnp.where(cid < valid, ig_r, NEG)
        lf_c = jnp.where(rid < valid, lf_c, 0.0)
        lf_r = jnp.where(cid < valid, lf_r, 0.0)
    causal = row >= col
    b_c = jnp.sum(jnp.where(causal, lf_r, 0.0), axis=1, keepdims=True)
    b_r = jnp.sum(jnp.where(row <= col, lf_c, 0.0), axis=0, keepdims=True)
    b_last = jnp.sum(lf_r, axis=1, keepdims=True)

    m_st = m_sc[...]
    d = jnp.where(causal, b_c - b_r + ig_r, NEG)
    m_inter = b_c + m_st
    m_t = jnp.maximum(m_inter, jnp.max(d, axis=1, keepdims=True))
    q = q_ref[...]
    s = lax.dot_general(q, k_ref[...], (((1,), (1,)), ((), ())), preferred_element_type=F32)
    s = s * jnp.exp(d - m_t)
    w_inter = jnp.exp(m_inter - m_t)
    c_st = c_sc[...]
    n_st = n_sc[...]
    v = v_ref[...]
    num = (jnp.dot(s.astype(BF16), v, preferred_element_type=F32)
           + w_inter * jnp.dot(q, c_st.astype(BF16), preferred_element_type=F32))
    qn = jnp.sum(q.astype(F32) * n_st, axis=1, keepdims=True)
    den = jnp.sum(s, axis=1, keepdims=True) + w_inter * qn
    hcell = num / jnp.maximum(jnp.abs(den), jnp.exp(-m_t))

    m_new = m_t[lc - 1:lc, :]
    w_r = jnp.exp(b_last - b_r + ig_r - m_new)
    w_c = jnp.exp(b_last - b_c + ig_c - m_new)
    decay = jnp.exp(b_last + m_st - m_new)
    kw_t = (kt_ref[0].astype(F32) * w_r).astype(BF16)
    c_new = decay * c_st + jnp.dot(kw_t, v, preferred_element_type=F32)
    n_new = decay * n_st + jnp.sum(k_ref[...].astype(F32) * w_c, axis=0, keepdims=True)
    c_sc[...] = c_new
    n_sc[...] = n_new
    m_sc[...] = m_new

    mu = jnp.mean(hcell, axis=1, keepdims=True)
    hc = hcell - mu
    var = jnp.mean(hc * hc, axis=1, keepdims=True)
    hn = hc * lax.rsqrt(var + EPS) * mh_ref[...]
    z = z_ref[...]
    o_ref[...] = ((hn + skip_ref[...] * xc_ref[...].astype(F32)) * (z * _sigmoid(z))).astype(BF16)

    @pl.when(c == pl.num_programs(2) - 1)
    def _():
        cout_ref[0, 0, 0] = c_new
        nout_ref[0, 0] = n_new
        mout_ref[0, 0] = jnp.broadcast_to(m_new, (1, 128))


def mlstm_cell(q, k, kt, v, xc, up, gates, lw, nseq, nchunk, lc, valid, layer, nlayer, init, c_prev):
    hd = A_HEAD_DIM
    nt = nseq * nchunk
    g = gates[:, :2 * A_HEADS].reshape(nt, lc, 2 * A_HEADS)
    g_col = jnp.transpose(g, (0, 2, 1))[..., None]
    g_row = jnp.transpose(g, (0, 2, 1))[:, :, None, :]
    row_spec = pl.BlockSpec((lc, hd), lambda s, h, c: (s * nchunk + c, h))
    in_specs = [row_spec, row_spec,
                pl.BlockSpec((1, hd, lc), lambda s, h, c: (s * nchunk + c, h, 0)),
                row_spec, row_spec,
                pl.BlockSpec((lc, hd), lambda s, h, c: (s * nchunk + c, A_HEADS + h)),
                pl.BlockSpec((1, 1, lc, 1), lambda s, h, c: (s * nchunk + c, h, 0, 0)),
                pl.BlockSpec((1, 1, lc, 1), lambda s, h, c: (s * nchunk + c, A_HEADS + h, 0, 0)),
                pl.BlockSpec((1, 1, 1, lc), lambda s, h, c: (s * nchunk + c, h, 0, 0)),
                pl.BlockSpec((1, 1, 1, lc), lambda s, h, c: (s * nchunk + c, A_HEADS + h, 0, 0)),
                pl.BlockSpec((1, hd), lambda s, h, c: (0, h)),
                pl.BlockSpec((1, hd), lambda s, h, c: (0, h))]
    args = [q, k, kt, v, xc, up, g_col, g_col, g_row, g_row, lw["mh_w"], lw["skip"]]
    has_init = init is not None
    if has_init:
        c0, n0, m0 = init
        in_specs += [pl.BlockSpec((1, 1, 1, hd, hd), lambda s, h, c: (layer, s, h, 0, 0)),
                     pl.BlockSpec((1, 1, 1, 1, hd), lambda s, h, c: (layer, s, h, 0, 0)),
                     pl.BlockSpec((1, 1, 1, 1, 1), lambda s, h, c: (layer, s, h, 0, 0))]
        args += [c0, n0.reshape(n0.shape[:3] + (1, hd)), m0.reshape(m0.shape + (1, 1))]
    has_prev = c_prev is not None
    aliases = {}
    if has_prev:
        aliases = {len(args): 1}
        in_specs.append(pl.BlockSpec(memory_space=pl.ANY))
        args.append(c_prev)
    out, c_out, n_out, m_out = pl.pallas_call(
        functools.partial(_mlstm_cell_kernel, lc=lc, valid=valid, has_init=has_init, has_prev=has_prev),
        grid=(nseq, A_HEADS, nchunk),
        in_specs=in_specs,
        out_specs=[row_spec,
                   pl.BlockSpec((1, 1, 1, hd, hd), lambda s, h, c: (layer, s, h, 0, 0)),
                   pl.BlockSpec((1, 1, 1, hd), lambda s, h, c: (s, h, 0, 0)),
                   pl.BlockSpec((1, 1, 1, 128), lambda s, h, c: (s, h, 0, 0))],
        out_shape=[jax.ShapeDtypeStruct((nt * lc, A_INNER), BF16),
                   jax.ShapeDtypeStruct((nlayer, nseq, A_HEADS, hd, hd), F32),
                   jax.ShapeDtypeStruct((nseq, A_HEADS, 1, hd), F32),
                   jax.ShapeDtypeStruct((nseq, A_HEADS, 1, 128), F32)],
        scratch_shapes=[pltpu.VMEM((hd, hd), F32), pltpu.VMEM((1, hd), F32), pltpu.VMEM((1, 1), F32)],
        input_output_aliases=aliases,
        compiler_params=_params(("parallel", "parallel", "arbitrary"), 40),
        name="mlstm_cell",
    )(*args)
    return out, c_out, n_out[:, :, 0, :], m_out[:, :, 0, 0]


def _mlstm_weights(layer, a_norm_pre, a_w_up, a_conv_w, a_conv_b, a_w_q, a_w_k, a_w_v, a_w_if, a_b_if,
                   a_mh_w, a_skip, a_w_down, a_norm_post):
    wif = a_w_if[layer].reshape(3, A_INNER, 2 * A_HEADS)
    wif = jnp.pad(wif, ((0, 0), (0, 0), (0, 128 - 2 * A_HEADS))).astype(BF16)
    return {
        "norm_pre": a_norm_pre[layer],
        "w_up": a_w_up[layer].astype(BF16),
        "conv_w8": jnp.pad(a_conv_w[layer], ((0, 8 - A_CONV), (0, 0))),
        "conv_b": a_conv_b[layer].reshape(1, A_INNER),
        "bd": jnp.stack([_headwise_tiles(a_w_q[layer]), _headwise_tiles(a_w_k[layer]),
                         _headwise_tiles(a_w_v[layer])]).astype(BF16),
        "wif": wif,
        "bif": jnp.pad(a_b_if[layer], (0, 128 - 2 * A_HEADS)).reshape(1, 128),
        "mh_w": a_mh_w[layer].reshape(1, A_INNER),
        "skip": a_skip[layer].reshape(1, A_INNER),
        "w_down": a_w_down[layer].astype(BF16),
        "norm_post": a_norm_post[layer],
    }


def mlstm_layer_prompt(x, lw, nseq, seq, layer, nlayer, c_prev):
    lc = CHUNK_P
    nchunk = seq // lc
    up = norm_matmul(x, lw["norm_pre"], lw["w_up"], 512, 1024)
    q, k, v, xc, gates, kt = mlstm_pre(up, up, nseq, nchunk, lc, 1, lw, zero_first=True, emit_kt=True)
    out, c, n, m = mlstm_cell(q, k, kt, v, xc, up, gates, lw, nseq, nchunk, lc, lc, layer, nlayer, None, c_prev)
    x_new = matmul_norm_res(out, lw["w_down"], lw["norm_post"], x, 512)
    conv_new = up.reshape(nseq, seq, 2 * A_INNER)[:, seq - (A_CONV - 1):, :A_INNER]
    return x_new, c, n, m, conv_new


def mlstm_layer_sample(x, lw, nseq, ds, layer, state, conv0, c_prev):
    nsub = 32
    assert nseq % nsub == 0
    up = norm_matmul(x, lw["norm_pre"], lw["w_up"], 512, 1024)
    halo = jnp.pad(conv0, ((0, 0), (8 - (A_CONV - 1), 0), (0, 0))).reshape(nseq * 8, A_INNER)
    q, k, v, xc, gates = mlstm_pre(up, halo, nseq // nsub, 1, SPAD, nsub, lw, zero_first=False, emit_kt=False)
    kt = jnp.transpose(k.reshape(nseq, SPAD, A_INNER), (0, 2, 1))
    out, c, n, m = mlstm_cell(q, k, kt, v, xc, up, gates, lw, nseq, 1, SPAD, ds, layer, state[0].shape[0],
                              state, c_prev)
    x_new = matmul_norm_res(out, lw["w_down"], lw["norm_post"], x, 512)
    xm = up.reshape(nseq, SPAD, 2 * A_INNER)[:, :ds, :A_INNER]
    conv_new = jnp.concatenate([conv0, xm], axis=1)[:, ds:]
    return x_new, c, n, m, conv_new


CMP_PACK = 4


def _compress_kernel(pt_ref, *refs, npages, feature_major):
    del pt_ref
    pages = refs[:npages]
    pos_ref, w1_ref, b1_ref, w2_ref, b2_ref, o_ref, stage_ref, acc_ref = refs[npages:]
    nblk = acc_ref.shape[1]
    ncb = stage_ref.shape[0]
    for p in range(npages):
        for cb in range(ncb):
            if feature_major:
                blk = pages[p][0, cb * 128:(cb + 1) * 128, :].T
            else:
                blk = pages[p][0, :, cb * 128:(cb + 1) * 128]
            stage_ref[cb, p * PAGE:(p + 1) * PAGE, :] = blk

    low = _low_half(nblk)
    for quad in range(CMP_STRIDE // CMP_PACK):
        for cb in range(ncb):
            xs = [stage_ref[cb, pl.ds(CMP_PACK * quad + i, nblk, stride=CMP_STRIDE), :] for i in range(CMP_PACK)]
            for gg in range(2):
                if gg == 0:
                    tiles = [jnp.where(low, xs[i], pltpu.roll(xs[i + 1], B_HEAD_DIM, axis=1))
                             for i in range(0, CMP_PACK, 2)]
                else:
                    tiles = [jnp.where(low, pltpu.roll(xs[i], B_HEAD_DIM, axis=1), xs[i + 1])
                             for i in range(0, CMP_PACK, 2)]
                packed = jnp.concatenate(tiles, axis=1)
                c = 2 * cb + gg
                slot = c // B_KV_HEADS
                for half in range(2):
                    idx = half * (CMP_STRIDE // CMP_PACK) + quad
                    xb = (packed + pos_ref[slot, idx]).astype(BF16)
                    d = jnp.dot(xb, w1_ref[slot, idx], preferred_element_type=F32)
                    if quad == 0:
                        acc_ref[2 * c + half] = d
                    else:
                        acc_ref[2 * c + half] += d

    out = jnp.zeros((nblk, 2 * B_KV_HEADS * B_HEAD_DIM), F32)
    for c in range(2 * B_KV_HEADS):
        slot = c // B_KV_HEADS
        nxt = pltpu.roll(acc_ref[2 * c + 1], nblk - 1, axis=0)
        hpre = acc_ref[2 * c] + nxt + b1_ref[slot]
        hid = hpre * _sigmoid(hpre)
        out = out + jnp.dot(hid.astype(BF16), w2_ref[c], preferred_element_type=F32)
    out = out + b2_ref[...]
    rid = lax.broadcasted_iota(jnp.int32, (nblk, 1), 0)
    o_ref[0] = jnp.where(rid < nblk - 1, out, 0.0)


def compress(pages_arr, table, nb, cw, feature_major):
    npages = table.shape[0] // nb
    nblk = npages * (PAGE // CMP_STRIDE)
    width = 2 * B_KV_HEADS * B_HEAD_DIM
    npk = CMP_LEN // CMP_PACK

    def page_map(b, pt, p):
        return (pt[b * npages + p], 0, 0)

    def const(shape):
        return pl.BlockSpec(shape, lambda b, pt: (0,) * len(shape))

    page_block = (1, width, PAGE) if feature_major else (1, PAGE, width)
    grid_spec = pltpu.PrefetchScalarGridSpec(
        num_scalar_prefetch=1,
        grid=(nb,),
        in_specs=[pl.BlockSpec(page_block, functools.partial(page_map, p=p)) for p in range(npages)]
        + [const((2, npk, 1, CMP_PACK * B_HEAD_DIM)), const((2, npk, CMP_PACK * B_HEAD_DIM, CMP_HIDDEN)),
           const((2, 1, CMP_HIDDEN)), const((2 * B_KV_HEADS, CMP_HIDDEN, width)), const((1, width))],
        out_specs=pl.BlockSpec((1, nblk, width), lambda b, pt: (b, 0, 0)),
        scratch_shapes=[pltpu.VMEM((width // 128, npages * PAGE, 128), F32),
                        pltpu.VMEM((4 * B_KV_HEADS, nblk, CMP_HIDDEN), F32)],
    )
    return pl.pallas_call(
        functools.partial(_compress_kernel, npages=npages, feature_major=feature_major),
        grid_spec=grid_spec,
        out_shape=jax.ShapeDtypeStruct((nb, nblk, width), F32),
        compiler_params=_params(("arbitrary",), 48),
        name="nsa_compress",
    )(table, *([pages_arr] * npages), cw["pos"], cw["w1"], cw["b1"], cw["w2p"], cw["b2p"])


def _compress_weights(cmp_pos, cmp_w1, cmp_b1, cmp_w2, cmp_b2):
    width = 2 * B_KV_HEADS * B_HEAD_DIM
    npk = CMP_LEN // CMP_PACK
    w2p = jnp.zeros((2 * B_KV_HEADS, CMP_HIDDEN, width), F32)
    for c in range(2 * B_KV_HEADS):
        w2p = w2p.at[c, :, c * B_HEAD_DIM:(c + 1) * B_HEAD_DIM].set(cmp_w2[c // B_KV_HEADS])
    return {
        "pos": cmp_pos.reshape(2, npk, 1, CMP_PACK * B_HEAD_DIM),
        "w1": cmp_w1.reshape(2, npk, CMP_PACK * B_HEAD_DIM, CMP_HIDDEN).astype(BF16),
        "b1": cmp_b1.reshape(2, 1, CMP_HIDDEN),
        "w2p": w2p.astype(BF16),
        "b2p": jnp.tile(cmp_b2[:, None, :], (1, B_KV_HEADS, 1)).reshape(1, width),
    }


def _softmax_masked(s, mask):
    s = jnp.where(mask, s, NEG)
    e = jnp.exp(s - jnp.max(s, axis=1, keepdims=True))
    return jnp.where(mask, e / jnp.sum(e, axis=1, keepdims=True), 0.0)


def _dot_nt(a, b):
    return lax.dot_general(a, b, (((1,), (1,)), ((), ())), preferred_element_type=F32)


def _low_half(tq):
    return lax.broadcasted_iota(jnp.int32, (tq, 128), 1) < B_HEAD_DIM


def _pad_heads(qblk, gl, tq):
    low = _low_half(tq)
    keep = low if gl == 0 else jnp.logical_not(low)
    tiles = []
    for i in range(4):
        hh = gl * 4 + i
        tile = qblk[:, (hh // 2) * 128:(hh // 2 + 1) * 128]
        if hh % 2 != gl:
            tile = pltpu.roll(tile, B_HEAD_DIM, axis=1)
        tiles.append(jnp.where(keep, tile, 0.0))
    return jnp.concatenate(tiles, axis=0).astype(BF16)


def _select_blocks(psum, qpos1, cover, ns):
    hi = psum.astype(BF16)
    lo = (psum - hi.astype(F32)).astype(BF16)
    imp = jnp.dot(hi, cover, preferred_element_type=F32) + jnp.dot(lo, cover, preferred_element_type=F32)
    s_idx = lax.broadcasted_iota(jnp.int32, (1, 128), 1)
    cur = jnp.right_shift(qpos1, 6)
    assert SEL_LEN == 64
    valid = s_idx * SEL_LEN <= qpos1
    forced = (s_idx == 0) | (s_idx == cur) | (s_idx == cur - 1)
    score = jnp.where(valid, imp + jnp.where(forced, FORCE_BONUS, 0.0), NEG)
    score = jnp.where(s_idx < ns, score, 2 * NEG)
    rank = jnp.zeros_like(score)
    for sp in range(ns):
        colv = score[:, sp:sp + 1]
        ahead = (colv > score) | ((colv == score) & (s_idx > sp))
        rank = rank + jnp.where(ahead, 1.0, 0.0)
    return jnp.where((rank < min(SEL_TOPN, ns)) & (s_idx < ns), 1.0, 0.0)


def _select_blocks_t(psum, qpos_row, cover_t, ns):
    tq = psum.shape[0]
    pt = psum.T
    hi = pt.astype(BF16)
    lo = (pt - hi.astype(F32)).astype(BF16)
    imp = jnp.dot(cover_t, hi, preferred_element_type=F32) + jnp.dot(cover_t, lo, preferred_element_type=F32)
    nr = -(-ns // 8) * 8
    imp = imp[0:nr]
    s_idx = lax.broadcasted_iota(jnp.int32, (nr, 1), 0)
    cur = jnp.right_shift(qpos_row, 6)
    valid = s_idx * SEL_LEN <= qpos_row
    forced = (s_idx == 0) | (s_idx == cur) | (s_idx == cur - 1)
    score = jnp.where(valid, imp + jnp.where(forced, FORCE_BONUS, 0.0), NEG)
    score = jnp.where(s_idx < ns, score, 2 * NEG)
    rank = jnp.zeros_like(score)
    for sp in range(ns):
        rowv = score[sp:sp + 1, :]
        ahead = (rowv > score) | ((rowv == score) & (s_idx > sp))
        rank = rank + jnp.where(ahead, 1.0, 0.0)
    sel_t = jnp.where((rank < min(SEL_TOPN, ns)) & (s_idx < ns), 1.0, 0.0)
    return jnp.concatenate([sel_t, jnp.zeros((128 - nr, tq), F32)], axis=0).T


def _cmp_branch(qs, tq, qpos, kc, vc):
    n_idx = lax.broadcasted_iota(jnp.int32, (1, 128), 1)
    p_c = _softmax_masked(_dot_nt(qs, kc), n_idx * CMP_STRIDE + (CMP_LEN - 1) <= qpos)
    o_c = jnp.dot(p_c.astype(BF16), vc, preferred_element_type=F32)
    return o_c, p_c[0:tq] + p_c[tq:2 * tq] + p_c[2 * tq:3 * tq] + p_c[3 * tq:4 * tq]


def _softmax_pv(s, tq, bias, pv):
    nk = s.shape[1]
    s3 = s.reshape(4, tq, nk) + bias[None]
    e = jnp.exp(s3 - jnp.max(s3, axis=2, keepdims=True))
    den = jnp.sum(e, axis=2, keepdims=True)
    o = pv(e.reshape(4 * tq, nk).astype(BF16))
    return o * (1.0 / den.reshape(4 * tq, 1))


def _combine(o_heads, sg, zs, tq):
    low = _low_half(tq)
    y = jnp.zeros((tq, 512), F32)
    for br in range(3):
        tiles = []
        for j in range(4):
            c = br * 8 + 2 * j
            a = o_heads[br][2 * j] * sg[:, c:c + 1]
            b = o_heads[br][2 * j + 1] * sg[:, c + 1:c + 2]
            if j // 2 == 0:
                tiles.append(jnp.where(low, a, pltpu.roll(b, B_HEAD_DIM, axis=1)))
            else:
                tiles.append(jnp.where(low, pltpu.roll(a, B_HEAD_DIM, axis=1), b))
        z = zs[br]
        y = y + jnp.concatenate(tiles, axis=1) * (z * _sigmoid(z))
    return y


def _nsa_prompt_kernel(q_ref, z0_ref, z1_ref, z2_ref, gate_ref, kc_ref, vc_ref, ks_ref, vs_ref, kw_ref, vw_ref,
                       cover_t_ref, blockneg_ref, y_ref, osel_sc, kaug_sc, *, tq, seq, kstep):
    q0 = pl.program_id(2) * tq
    qblk = q_ref[...] * (B_HEAD_DIM ** -0.5)
    qpos1 = q0 + lax.broadcasted_iota(jnp.int32, (tq, 1), 0)
    qpos_row = q0 + lax.broadcasted_iota(jnp.int32, (1, tq), 1)
    qpos = jnp.concatenate([qpos1] * 4, axis=0)
    wlen = WINDOW + tq
    wstart = pl.multiple_of(jnp.maximum(q0 - WINDOW, 0), 128)
    kw = kw_ref[pl.ds(wstart, wlen), :]
    vw = vw_ref[pl.ds(wstart, wlen), :]
    kpos_w = wstart + lax.broadcasted_iota(jnp.int32, (1, wlen), 1)
    wbias = jnp.where((kpos_w <= qpos1) & (kpos_w > qpos1 - WINDOW), 0.0, NEG)
    kc = kc_ref[0].astype(BF16)
    vc = vc_ref[0].astype(BF16)

    @pl.when(pl.program_id(2) == 0)
    def _():
        kaug_sc[:, 0:128] = ks_ref[...]
        kaug_sc[:, 128:256] = blockneg_ref[...]

    nbefore = (q0 + kstep - 1) // kstep
    first_blk = q0 // SEL_LEN
    kd = ks_ref[pl.ds(pl.multiple_of(q0, 128), tq), :]
    vd = vs_ref[pl.ds(pl.multiple_of(q0, 128), tq), :]
    tri = jnp.where(lax.broadcasted_iota(jnp.int32, (tq, tq), 1) <= lax.broadcasted_iota(jnp.int32, (tq, tq), 0),
                    0.0, NEG)
    s_idx = lax.broadcasted_iota(jnp.int32, (1, 128), 1)
    o_heads = [[None] * 8 for _ in range(3)]
    for gl in range(2):
        qs = _pad_heads(qblk, gl, tq)
        o_c, psum = _cmp_branch(qs, tq, qpos, kc, vc)
        sel = _select_blocks_t(psum, qpos_row, cover_t_ref[...], seq // SEL_LEN)
        skip = jnp.where((sel > 0.5) & (s_idx < first_blk), 0.0, 1.0).astype(BF16)
        q_aug = jnp.concatenate([qs, jnp.concatenate([skip] * 4, axis=0)], axis=1)
        s_d = _dot_nt(qs, kd).reshape(4, tq, tq) + tri[None]
        m_d = jnp.max(s_d, axis=2, keepdims=True)

        @pl.when(nbefore == 0)
        def _(gl=gl, s_d=s_d, m_d=m_d):
            e_d = jnp.exp(s_d - m_d)
            den = jnp.sum(e_d, axis=2, keepdims=True)
            o = jnp.dot(e_d.reshape(4 * tq, tq).astype(BF16), vd, preferred_element_type=F32)
            osel_sc[gl] = o * (1.0 / den.reshape(4 * tq, 1))

        for v in range(1, seq // kstep + 1):
            @pl.when(nbefore == v)
            def _(nk=v * kstep, gl=gl, q_aug=q_aug, s_d=s_d, m_d=m_d):
                s_b = _dot_nt(q_aug, kaug_sc[0:nk, :]).reshape(4, tq, nk)
                m = jnp.maximum(m_d, jnp.max(s_b, axis=2, keepdims=True))
                e_b = jnp.exp(s_b - m)
                e_d = jnp.exp(s_d - m)
                den = jnp.sum(e_b, axis=2, keepdims=True) + jnp.sum(e_d, axis=2, keepdims=True)
                o = (jnp.dot(e_b.reshape(4 * tq, nk).astype(BF16), vs_ref[0:nk, :], preferred_element_type=F32)
                     + jnp.dot(e_d.reshape(4 * tq, tq).astype(BF16), vd, preferred_element_type=F32))
                osel_sc[gl] = o * (1.0 / den.reshape(4 * tq, 1))

        o_w = _softmax_pv(_dot_nt(qs, kw), tq, wbias, lambda e: jnp.dot(e, vw, preferred_element_type=F32))
        outs = (o_c, osel_sc[gl], o_w)
        for br in range(3):
            for i in range(4):
                o_heads[br][gl * 4 + i] = outs[br][i * tq:(i + 1) * tq]
    y = _combine(o_heads, _sigmoid(gate_ref[...]), (z0_ref[...], z1_ref[...], z2_ref[...]), tq)
    y_ref[...] = y.astype(BF16)


def _cover_matrix(transposed=False):
    n = np.arange(128)[:, None]
    s = np.arange(128)[None, :]
    cov = (n * CMP_STRIDE < s * SEL_LEN + SEL_LEN) & (n * CMP_STRIDE + CMP_LEN - 1 >= s * SEL_LEN)
    return jnp.asarray(cov.T if transposed else cov, BF16)


def _expand_matrix(nkeys):
    s = np.arange(128)[:, None]
    key = np.arange(nkeys)[None, :]
    return jnp.asarray(key // SEL_LEN == s, BF16)


def nsa_attend_prompt(pm, pg, cmp, kvb, nb, seq):
    tq = 128
    nq = seq // tq

    def rows(width, col):
        return pl.BlockSpec((tq, width), lambda b, gp, t: (b * nq + t, col(gp)))

    def seq_block(first):
        return pl.BlockSpec((seq, 128), lambda b, gp, t: (b, first + gp))

    kstep = 256
    assert seq % kstep == 0 and tq % 128 == 0
    return pl.pallas_call(
        functools.partial(_nsa_prompt_kernel, tq=tq, seq=seq, kstep=kstep),
        grid=(nb, 2, nq),
        in_specs=[rows(512, lambda gp: gp), rows(512, lambda gp: 2 + gp), rows(512, lambda gp: 4 + gp),
                  rows(512, lambda gp: 6 + gp), rows(128, lambda gp: gp),
                  pl.BlockSpec((1, 128, 128), lambda b, gp, t: (b, 0, gp)),
                  pl.BlockSpec((1, 128, 128), lambda b, gp, t: (b, 0, 2 + gp)),
                  seq_block(4), seq_block(6), seq_block(8), seq_block(10),
                  pl.BlockSpec((128, 128), lambda b, gp, t: (0, 0)),
                  pl.BlockSpec((seq, 128), lambda b, gp, t: (0, 0))],
        out_specs=rows(512, lambda gp: gp),
        out_shape=jax.ShapeDtypeStruct((nb * seq, B_ATT), BF16),
        scratch_shapes=[pltpu.VMEM((2, 4 * tq, 128), F32), pltpu.VMEM((seq, 256), BF16)],
        compiler_params=_params(("parallel", "parallel", "arbitrary"), 56),
        name="nsa_attend_prompt",
    )(pm, pm, pm, pm, pg, cmp, cmp, kvb, kvb, kvb, kvb, _cover_matrix(transposed=True),
      NEG * _expand_matrix(seq).T)


def _nsa_sample_kernel(pt_ref, *refs, npages, past):
    del pt_ref
    pages = refs[:npages]
    (q_ref, z0_ref, z1_ref, z2_ref, gate_ref, cmp_ref, new_ref, win_ref, cover_ref, expand_ref, y_ref,
     kst_sc, vst_sc) = refs[npages:]
    tq = SPAD
    ngp = 2
    rows_gp = 8 * tq
    wrows = win_ref.shape[2]
    ns = past // SEL_LEN + 1
    sg = _sigmoid(gate_ref[...])

    def qpos_rows(n):
        return past + jnp.bitwise_and(lax.broadcasted_iota(jnp.int32, (n, 1), 0), tq - 1)

    def new_rows(col):
        blk = new_ref[:, col:col + 128].astype(BF16)
        return jnp.concatenate([blk, jnp.zeros((NEW_PAD - tq, 128), BF16)], axis=0)

    def per_pair(fn):
        return jnp.concatenate([fn(gp) for gp in range(ngp)], axis=0)

    for gp in range(ngp):
        lo, hi = gp * 128, (gp + 1) * 128
        for p in range(npages):
            kst_sc[gp, :, p * PAGE:(p + 1) * PAGE] = pages[p][0, lo:hi, :].astype(BF16)
            vst_sc[gp, :, p * PAGE:(p + 1) * PAGE] = pages[p][0, 256 + lo:256 + hi, :].astype(BF16)
    qs = []
    for gp in range(ngp):
        qblk = q_ref[:, gp * 512:(gp + 1) * 512] * (B_HEAD_DIM ** -0.5)
        qs.append(jnp.concatenate([_pad_heads(qblk, 0, tq), _pad_heads(qblk, 1, tq)], axis=0))

    n_idx = lax.broadcasted_iota(jnp.int32, (1, 128), 1)
    s = per_pair(lambda gp: _dot_nt(qs[gp], cmp_ref[0, :, gp * 128:(gp + 1) * 128].astype(BF16)))
    p_c = _softmax_masked(s, n_idx * CMP_STRIDE + (CMP_LEN - 1) <= qpos_rows(ngp * rows_gp))
    p_cb = p_c.astype(BF16)
    o_c = [jnp.dot(p_cb[gp * rows_gp:(gp + 1) * rows_gp],
                   cmp_ref[0, :, 256 + gp * 128:256 + (gp + 1) * 128].astype(BF16), preferred_element_type=F32)
           for gp in range(ngp)]
    psum = jnp.concatenate([sum(p_c[(g * 4 + i) * tq:(g * 4 + i + 1) * tq] for i in range(4))
                            for g in range(2 * ngp)], axis=0)
    nsel_rows = 2 * ngp * tq
    sel = _select_blocks(psum, qpos_rows(nsel_rows), cover_ref[...], ns).astype(BF16)
    selk = jnp.dot(sel, expand_ref[...], preferred_element_type=F32) > 0.5
    kpos_s = lax.broadcasted_iota(jnp.int32, (1, past + NEW_PAD), 1)
    sbias = jnp.where(selk & (kpos_s <= qpos_rows(nsel_rows)), 0.0, NEG)
    sbias = jnp.concatenate([sbias[g * tq:(g + 1) * tq] for g in range(2 * ngp) for _ in range(4)], axis=0)
    s = per_pair(lambda gp: jnp.concatenate(
        [jnp.dot(qs[gp], kst_sc[gp], preferred_element_type=F32),
         _dot_nt(qs[gp], new_rows(512 + gp * 128))], axis=1)) + sbias
    e = jnp.exp(s - jnp.max(s, axis=1, keepdims=True))
    rden = 1.0 / jnp.sum(e, axis=1, keepdims=True)
    e = e.astype(BF16)
    o_s = [(_dot_nt(e[gp * rows_gp:(gp + 1) * rows_gp, 0:past], vst_sc[gp])
            + jnp.dot(e[gp * rows_gp:(gp + 1) * rows_gp, past:], new_rows(768 + gp * 128),
                      preferred_element_type=F32)) * rden[gp * rows_gp:(gp + 1) * rows_gp]
           for gp in range(ngp)]
    kpos_w = (past - wrows) + lax.broadcasted_iota(jnp.int32, (1, wrows + NEW_PAD), 1)
    qpos = qpos_rows(ngp * rows_gp)
    wbias = jnp.where((kpos_w <= qpos) & (kpos_w > qpos - WINDOW) & (kpos_w >= 0), 0.0, NEG)
    s = per_pair(lambda gp: jnp.concatenate(
        [jnp.dot(qs[gp], win_ref[0, gp * 128:(gp + 1) * 128, :].astype(BF16), preferred_element_type=F32),
         _dot_nt(qs[gp], new_rows(1024 + gp * 128))], axis=1)) + wbias
    e = jnp.exp(s - jnp.max(s, axis=1, keepdims=True))
    rden = 1.0 / jnp.sum(e, axis=1, keepdims=True)
    e = e.astype(BF16)
    o_w = [(_dot_nt(e[gp * rows_gp:(gp + 1) * rows_gp, 0:wrows],
                    win_ref[0, 256 + gp * 128:256 + (gp + 1) * 128, :].astype(BF16))
            + jnp.dot(e[gp * rows_gp:(gp + 1) * rows_gp, wrows:], new_rows(1280 + gp * 128),
                      preferred_element_type=F32)) * rden[gp * rows_gp:(gp + 1) * rows_gp]
           for gp in range(ngp)]

    for gp in range(ngp):
        o_heads = [[outs[gp][hh * tq:(hh + 1) * tq] for hh in range(8)] for outs in (o_c, o_s, o_w)]
        zs = (z0_ref[:, gp * 512:(gp + 1) * 512], z1_ref[:, gp * 512:(gp + 1) * 512],
              z2_ref[:, gp * 512:(gp + 1) * 512])
        y_ref[:, gp * 512:(gp + 1) * 512] = _combine(o_heads, sg[:, gp * 128:(gp + 1) * 128], zs, tq).astype(BF16)


def nsa_attend_sample(pm, pg, cmp, kv_new, cache_pages, table, cache_win, nb, past):
    npages = past // PAGE
    wrows = cache_win.shape[2]
    nks = past + NEW_PAD
    assert past % SEL_LEN == 0 and SPAD <= SEL_LEN

    def page_map(b, pt, p):
        return (pt[b * npages + p], 1, 0)

    def rows(width, col):
        return pl.BlockSpec((SPAD, width), lambda b, pt: (b, col))

    grid_spec = pltpu.PrefetchScalarGridSpec(
        num_scalar_prefetch=1,
        grid=(nb,),
        in_specs=[pl.BlockSpec((1, 512, PAGE), functools.partial(page_map, p=p)) for p in range(npages)]
        + [rows(B_ATT, 0), rows(B_ATT, 1), rows(B_ATT, 2), rows(B_ATT, 3), rows(256, 0),
           pl.BlockSpec((1, 128, 512), lambda b, pt: (b, 0, 0)),
           rows(1536, 0),
           pl.BlockSpec((1, 512, wrows), lambda b, pt: (b, 0, 0)),
           pl.BlockSpec((128, 128), lambda b, pt: (0, 0)),
           pl.BlockSpec((128, nks), lambda b, pt: (0, 0))],
        out_specs=rows(B_ATT, 0),
        scratch_shapes=[pltpu.VMEM((2, 128, past), BF16), pltpu.VMEM((2, 128, past), BF16)],
    )
    return pl.pallas_call(
        functools.partial(_nsa_sample_kernel, npages=npages, past=past),
        grid_spec=grid_spec,
        out_shape=jax.ShapeDtypeStruct((nb * SPAD, B_ATT), BF16),
        compiler_params=_params(("arbitrary",), 48),
        name="nsa_attend_sample",
    )(table, *([cache_pages] * npages), pm, pm, pm, pm, pg, cmp, kv_new, cache_win,
      _cover_matrix(), _expand_matrix(nks))


def _nsa_weights(lb, b_norm_pre, b_w_in, b_w_out, b_norm_post):
    w_in = b_w_in[lb]
    wg = w_in[:, 4 * B_ATT:].reshape(D_MODEL, 3, 2, 8)
    wg = jnp.transpose(wg, (0, 2, 1, 3)).reshape(D_MODEL, 2, 24)
    wg = jnp.pad(wg, ((0, 0), (0, 0), (0, 128 - 24))).reshape(D_MODEL, 256)
    return {"norm_pre": b_norm_pre[lb], "w_main": w_in[:, :4 * B_ATT].astype(BF16), "w_gate": wg.astype(BF16),
            "w_out": b_w_out[lb].astype(BF16), "norm_post": b_norm_post[lb]}


def kernel(x_prompt, x_sample, cache_kv, page_table, cache_win, state_C, state_n, state_m, state_conv,
           a_norm_pre, a_w_up, a_conv_w, a_conv_b, a_w_q, a_w_k, a_w_v, a_w_if, a_b_if,
           a_mh_w, a_skip, a_w_down, a_norm_post,
           kv_norm, w_kv, cmp_pos, cmp_w1, cmp_b1, cmp_w2, cmp_b2,
           b_norm_pre, b_w_in, b_w_out, b_norm_post):
    bp, sp = x_prompt.shape[0], x_prompt.shape[1]
    bs, ds = x_sample.shape[0], x_sample.shape[1]
    n_a = state_C.shape[0]
    n_b = b_w_in.shape[0]
    past = page_table.shape[1] * cache_kv.shape[1]
    assert cache_kv.shape[1] == PAGE and sp % PAGE == 0 and ds <= SPAD
    assert (past + ds - CMP_LEN) // CMP_STRIDE + 1 == past // CMP_STRIDE - 1
    xp = x_prompt.reshape(bp * sp, D_MODEL)
    xs = jnp.pad(x_sample, ((0, 0), (0, SPAD - ds), (0, 0))).reshape(bs * SPAD, D_MODEL)

    np_l, ns_l, mp_l, ms_l, vp_l, vs_l = [], [], [], [], [], []
    c_prompt = c_sample = None
    for layer in range(n_a):
        lw = _mlstm_weights(layer, a_norm_pre, a_w_up, a_conv_w, a_conv_b, a_w_q, a_w_k, a_w_v, a_w_if, a_b_if,
                            a_mh_w, a_skip, a_w_down, a_norm_post)
        xp, c_prompt, n1, m1, v1 = mlstm_layer_prompt(xp, lw, bp, sp, layer, n_a, c_prompt)
        xs, c_sample, n2, m2, v2 = mlstm_layer_sample(xs, lw, bs, ds, layer, (state_C, state_n, state_m),
                                                      state_conv[layer], c_sample)
        np_l.append(n1); mp_l.append(m1); vp_l.append(v1)
        ns_l.append(n2); ms_l.append(m2); vs_l.append(v2)

    nkv = w_kv.shape[1]
    w_kv_b = w_kv.astype(BF16)
    kvp, kvp_b = norm_matmul(xp, kv_norm, w_kv_b, 512, 512, also_bf16=True)
    kvs = norm_matmul(xs, kv_norm, w_kv_b, 512, 512)
    cw = _compress_weights(cmp_pos, cmp_w1, cmp_b1, cmp_w2, cmp_b2)
    table_p = jnp.arange(bp * (sp // PAGE), dtype=jnp.int32)
    table_s = page_table.reshape(-1).astype(jnp.int32)
    cache_pages = jnp.transpose(cache_kv, (0, 2, 3, 4, 1)).reshape(cache_kv.shape[0], -1, PAGE)
    cmp_p = compress(kvp.reshape(bp * sp // PAGE, PAGE, nkv), table_p, bp, cw, feature_major=False)
    cmp_s = compress(cache_pages, table_s, bs, cw, feature_major=True)
    wb = cache_win.shape[1]
    win2 = jnp.transpose(cache_win, (0, 2, 3, 4, 1)).reshape(bs, -1, wb)

    for lb in range(n_b):
        bw = _nsa_weights(lb, b_norm_pre, b_w_in, b_w_out, b_norm_post)
        pm = norm_matmul(xp, bw["norm_pre"], bw["w_main"], 512, 1024)
        pg = norm_matmul(xp, bw["norm_pre"], bw["w_gate"], 512, 256)
        yp = nsa_attend_prompt(pm, pg, cmp_p, kvp_b, bp, sp)
        xp = matmul_norm_res(yp, bw["w_out"], bw["norm_post"], xp, 512)
        pm = norm_matmul(xs, bw["norm_pre"], bw["w_main"], 512, 1024)
        pg = norm_matmul(xs, bw["norm_pre"], bw["w_gate"], 512, 256)
        ys = nsa_attend_sample(pm, pg, cmp_s, kvs, cache_pages, table_s, win2, bs, past)
        xs = matmul_norm_res(ys, bw["w_out"], bw["norm_post"], xs, 512)

    y_prompt = xp.reshape(bp, sp, D_MODEL)
    y_sample = xs.reshape(bs, SPAD, D_MODEL)[:, :ds]
    kvp3 = kvp.reshape(bp, sp, nkv)
    kvs3 = kvs.reshape(bs, SPAD, nkv)[:, :ds]
    ncache = 4 * B_KV_HEADS * B_HEAD_DIM
    kv_rows_prompt = kvp3[:, :, :ncache].reshape(bp, sp, 4, B_KV_HEADS, B_HEAD_DIM)
    kv_rows_sample = kvs3[:, :, :ncache].reshape(bs, ds, 4, B_KV_HEADS, B_HEAD_DIM)
    win_prompt = kvp3[:, sp - min(WINDOW, sp):, ncache:].reshape(bp, min(WINDOW, sp), 2, B_KV_HEADS, B_HEAD_DIM)
    win_new = kvs3[:, :, ncache:].reshape(bs, ds, 2, B_KV_HEADS, B_HEAD_DIM).astype(cache_win.dtype)
    win_sample = jnp.concatenate([cache_win, win_new], axis=1)[:, ds:]
    return (y_prompt, y_sample, kv_rows_prompt, kv_rows_sample, win_prompt, win_sample,
            c_prompt, c_sample, jnp.stack(np_l), jnp.stack(ns_l),
            jnp.stack(mp_l), jnp.stack(ms_l), jnp.stack(vp_l), jnp.stack(vs_l))
```

```python
import functools

import jax
import jax.numpy as jnp
import numpy as np
from jax import lax
from jax.experimental import pallas as pl
from jax.experimental.pallas import tpu as pltpu

F32 = jnp.float32
BF16 = jnp.bfloat16

D_MODEL = 1024
A_INNER = 2048
A_HEADS = 4
A_HEAD_DIM = 512
A_CONV = 4
B_HEADS = 16
B_HEAD_DIM = 64
B_KV_HEADS = 4
B_ATT = 1024
CMP_LEN = 32
CMP_STRIDE = 16
CMP_HIDDEN = 256
SEL_LEN = 64
SEL_TOPN = 16
WINDOW = 512
PAGE = 128
FORCE_BONUS = 1.0e3
NEG = -1.0e30
EPS = 1e-6

SPAD = 16
NEW_PAD = 128
CHUNK_P = 256
TM_PROJ = 1024
V7X_VMEM_BYTES = 64 * 2**20


def _params(sem, vmem_mb):
    assert vmem_mb * 2**20 < V7X_VMEM_BYTES
    return pltpu.CompilerParams(dimension_semantics=sem, vmem_limit_bytes=vmem_mb * 2**20)


def _sigmoid(x):
    return 1.0 / (1.0 + jnp.exp(-x))


def _norm_matmul_kernel(x_ref, nw_ref, w_ref, *rest):
    o_refs, xn_ref = rest[:-1], rest[-1]

    @pl.when(pl.program_id(1) == 0)
    def _():
        x = x_ref[...]
        ms = jnp.mean(x * x, axis=-1, keepdims=True)
        xn_ref[...] = (x * lax.rsqrt(ms + EPS) * nw_ref[...]).astype(BF16)

    y = jnp.dot(xn_ref[...], w_ref[...], preferred_element_type=F32)
    for o_ref in o_refs:
        o_ref[...] = y.astype(o_ref.dtype)


def norm_matmul(x, nw, w, tm, tn, also_bf16=False, out_dtype=F32):
    m, k = x.shape
    n = w.shape[1]
    assert m % tm == 0 and n % tn == 0
    o_spec = pl.BlockSpec((tm, tn), lambda i, j: (i, j))
    out_specs = [o_spec]
    out_shape = [jax.ShapeDtypeStruct((m, n), out_dtype)]
    if also_bf16:
        out_specs.append(o_spec)
        out_shape.append(jax.ShapeDtypeStruct((m, n), BF16))
    res = pl.pallas_call(
        _norm_matmul_kernel,
        grid=(m // tm, n // tn),
        in_specs=[pl.BlockSpec((tm, k), lambda i, j: (i, 0)),
                  pl.BlockSpec((1, k), lambda i, j: (0, 0)),
                  pl.BlockSpec((k, tn), lambda i, j: (0, j))],
        out_specs=out_specs,
        out_shape=out_shape,
        scratch_shapes=[pltpu.VMEM((tm, k), BF16)],
        compiler_params=_params(("parallel", "arbitrary"), 40),
        name="norm_matmul",
    )(x, nw.reshape(1, k), w)
    return res if also_bf16 else res[0]


def _matmul_norm_res_kernel(a_ref, w_ref, nw_ref, x_ref, o_ref):
    y = jnp.dot(a_ref[...], w_ref[...], preferred_element_type=F32)
    ms = jnp.mean(y * y, axis=-1, keepdims=True)
    o_ref[...] = x_ref[...] + y * lax.rsqrt(ms + EPS) * nw_ref[...]


def matmul_norm_res(a, w, nw, x, tm):
    m, k = a.shape
    n = w.shape[1]
    assert m % tm == 0
    return pl.pallas_call(
        _matmul_norm_res_kernel,
        grid=(m // tm,),
        in_specs=[pl.BlockSpec((tm, k), lambda i: (i, 0)),
                  pl.BlockSpec((k, n), lambda i: (0, 0)),
                  pl.BlockSpec((1, n), lambda i: (0, 0)),
                  pl.BlockSpec((tm, n), lambda i: (i, 0))],
        out_specs=pl.BlockSpec((tm, n), lambda i: (i, 0)),
        out_shape=jax.ShapeDtypeStruct((m, n), F32),
        compiler_params=_params(("parallel",), 40),
        name="matmul_norm_res",
    )(a, w, nw.reshape(1, n), x)


def _mlstm_pre_kernel(xm_ref, prev_ref, cw_ref, cb_ref, bd_ref, wif_ref, bif_ref,
                      q_ref, k_ref, v_ref, xc_ref, g_ref, *rest, lsub, nsub, zero_first, emit_kt):
    if emit_kt:
        kt_ref, cat_ref = rest
    else:
        (cat_ref,) = rest
    t = pl.program_id(1)
    h = pl.program_id(2)
    x = xm_ref[...]
    prev = prev_ref[...]
    if zero_first:
        prev = jnp.where(t == 0, 0.0, prev)
    cw = cw_ref[...]
    pieces = []
    for s in range(nsub):
        base = s * (lsub + 8)
        xs = x[s * lsub:(s + 1) * lsub]
        cat_ref[base:base + 8, :] = prev[s * 8:(s + 1) * 8]
        cat_ref[base + 8:base + 8 + lsub, :] = xs
        acc = cb_ref[...] + xs * cw[A_CONV - 1:A_CONV]
        for j in range(1, A_CONV):
            acc = acc + cat_ref[base + 8 - j:base + 8 - j + lsub, :] * cw[A_CONV - 1 - j:A_CONV - j]
        pieces.append(acc)
    xconv = pieces[0] if nsub == 1 else jnp.concatenate(pieces, axis=0)
    xc = xconv * _sigmoid(xconv)

    xcb = xc.astype(BF16)
    xb = x.astype(BF16)
    ntile = x.shape[1] // 128

    def headwise(src, which):
        return jnp.concatenate(
            [jnp.dot(src[:, j * 128:(j + 1) * 128], bd_ref[which, j], preferred_element_type=F32)
             for j in range(ntile)], axis=1)

    q = headwise(xcb, 0)
    k = headwise(xcb, 1)
    v = headwise(xb, 2)

    g = (jnp.dot(q.astype(BF16), wif_ref[0], preferred_element_type=F32)
         + jnp.dot(k.astype(BF16), wif_ref[1], preferred_element_type=F32)
         + jnp.dot(v.astype(BF16), wif_ref[2], preferred_element_type=F32))

    @pl.when(h == 0)
    def _():
        g_ref[...] = bif_ref[...] + g

    @pl.when(h != 0)
    def _():
        g_ref[...] += g

    ks = k * (A_HEAD_DIM ** -0.5)
    q_ref[...] = q.astype(BF16)
    k_ref[...] = ks.astype(BF16)
    v_ref[...] = v.astype(BF16)
    xc_ref[...] = xc.astype(BF16)
    if emit_kt:
        kt_ref[0] = ks.T.astype(BF16)


def _headwise_tiles(w):
    n, c, _ = w.shape
    per = 128 // c
    wt = w.reshape(n // per, per, c, c)
    eye = jnp.eye(per, dtype=w.dtype)
    return jnp.einsum("nm,tncd->tncmd", eye, wt).reshape(n // per, 128, 128)


def mlstm_pre(up, halo, nseq, ntile, lsub, nsub, lw, zero_first, emit_kt):
    lt = nsub * lsub
    rows = nseq * ntile * lt
    hd = A_HEAD_DIM
    if zero_first:
        assert nsub == 1

        def prev_map(s, t, h):
            return (jnp.maximum((s * ntile + t) * (lt // 8) - 1, 0), h)
    else:
        assert ntile == 1

        def prev_map(s, t, h):
            return (s, h)
    wspec = pl.BlockSpec((8, hd), lambda s, t, h: (0, h))
    row_spec = pl.BlockSpec((lt, hd), lambda s, t, h: (s * ntile + t, h))
    out_specs = [row_spec, row_spec, row_spec, row_spec,
                 pl.BlockSpec((lt, 128), lambda s, t, h: (s * ntile + t, 0))]
    out_shape = [jax.ShapeDtypeStruct((rows, A_INNER), BF16)] * 4 + [jax.ShapeDtypeStruct((rows, 128), F32)]
    if emit_kt:
        out_specs.append(pl.BlockSpec((1, hd, lt), lambda s, t, h: (s * ntile + t, h, 0)))
        out_shape.append(jax.ShapeDtypeStruct((nseq * ntile, A_INNER, lt), BF16))
    return pl.pallas_call(
        functools.partial(_mlstm_pre_kernel, lsub=lsub, nsub=nsub, zero_first=zero_first, emit_kt=emit_kt),
        grid=(nseq, ntile, A_HEADS),
        in_specs=[row_spec,
                  pl.BlockSpec((8 * nsub, hd), prev_map),
                  wspec,
                  pl.BlockSpec((1, hd), lambda s, t, h: (0, h)),
                  pl.BlockSpec((3, hd // 128, 128, 128), lambda s, t, h: (0, h, 0, 0)),
                  pl.BlockSpec((3, hd, 128), lambda s, t, h: (0, h, 0)),
                  pl.BlockSpec((1, 128), lambda s, t, h: (0, 0))],
        out_specs=out_specs,
        out_shape=out_shape,
        scratch_shapes=[pltpu.VMEM((nsub * (lsub + 8), hd), F32)],
        compiler_params=_params(("parallel", "parallel", "arbitrary"), 40),
        name="mlstm_pre",
    )(up, halo, lw["conv_w8"], lw["conv_b"], lw["bd"], lw["wif"], lw["bif"])


def _mlstm_cell_kernel(*refs, lc, valid, has_init, has_prev):
    (q_ref, k_ref, kt_ref, v_ref, xc_ref, z_ref, igr_ref, fgr_ref, mh_ref, skip_ref) = refs[:10]
    rest = refs[10:]
    if has_init:
        c0_ref, n0_ref, m0_ref = rest[:3]
        rest = rest[3:]
    if has_prev:
        rest = rest[1:]
    o_ref, cout_ref, nout_ref, mout_ref, c_sc, n_sc, m_sc = rest
    c = pl.program_id(2)

    @pl.when(c == 0)
    def _():
        if has_init:
            c_sc[...] = c0_ref[0, 0, 0]
            n_sc[...] = n0_ref[0, 0, 0]
            m_sc[...] = m0_ref[0, 0, 0]
        else:
            c_sc[...] = jnp.zeros_like(c_sc)
            n_sc[...] = jnp.zeros_like(n_sc)
            m_sc[...] = jnp.full_like(m_sc, NEG)

    def logsig(x):
        return jnp.minimum(x, 0.0) - jnp.log(1.0 + jnp.exp(-jnp.abs(x)))

    ig_r = igr_ref[0, 0]
    lf_r = logsig(fgr_ref[0, 0])
    row = lax.broadcasted_iota(jnp.int32, (lc, lc), 0)
    col = lax.broadcasted_iota(jnp.int32, (lc, lc), 1)
    if valid < lc:
        cid = lax.broadcasted_iota(jnp.int32, (1, lc), 1)
        ig_r = jnp.where(cid < valid, ig_r, NEG)
        lf_r = jnp.where(cid < valid, lf_r, 0.0)
    eye = row == col
    ig_c = jnp.sum(jnp.where(eye, ig_r, 0.0), axis=1, keepdims=True)
    lf_c = jnp.sum(jnp.where(eye, lf_r, 0.0), axis=1, keepdims=True)
    causal = row >= col
    b_c = jnp.sum(jnp.where(causal, lf_r, 0.0), axis=1, keepdims=True)
    b_r = jnp.sum(jnp.where(row <= col, lf_c, 0.0), axis=0, keepdims=True)
    b_last = jnp.sum(lf_r, axis=1, keepdims=True)

    m_st = m_sc[...]
    d = jnp.where(causal, b_c - b_r + ig_r, NEG)
    m_inter = b_c + m_st
    m_t = jnp.maximum(m_inter, jnp.max(d, axis=1, keepdims=True))
    q = q_ref[...]
    s = lax.dot_general(q, k_ref[...], (((1,), (1,)), ((), ())), preferred_element_type=F32)
    s = s * jnp.exp(d - m_t)
    w_inter = jnp.exp(m_inter - m_t)
    c_st = c_sc[...]
    n_st = n_sc[...]
    v = v_ref[...]
    num = (jnp.dot(s.astype(BF16), v, preferred_element_type=F32)
           + w_inter * jnp.dot(q, c_st.astype(BF16), preferred_element_type=F32))
    qn = jnp.sum(q.astype(F32) * n_st, axis=1, keepdims=True)
    den = jnp.sum(s, axis=1, keepdims=True) + w_inter * qn
    hcell = num / jnp.maximum(jnp.abs(den), jnp.exp(-m_t))

    m_new = m_t[lc - 1:lc, :]
    w_r = jnp.exp(b_last - b_r + ig_r - m_new)
    w_c = jnp.exp(b_last - b_c + ig_c - m_new)
    decay = jnp.exp(b_last + m_st - m_new)
    kw_t = (kt_ref[0].astype(F32) * w_r).astype(BF16)
    c_new = decay * c_st + jnp.dot(kw_t, v, preferred_element_type=F32)
    n_new = decay * n_st + jnp.sum(k_ref[...].astype(F32) * w_c, axis=0, keepdims=True)
    c_sc[...] = c_new
    n_sc[...] = n_new
    m_sc[...] = m_new

    mu = jnp.mean(hcell, axis=1, keepdims=True)
    hc = hcell - mu
    var = jnp.mean(hc * hc, axis=1, keepdims=True)
    hn = hc * lax.rsqrt(var + EPS) * mh_ref[...]
    z = z_ref[...]
    o_ref[...] = ((hn + skip_ref[...] * xc_ref[...].astype(F32)) * (z * _sigmoid(z))).astype(BF16)

    @pl.when(c == pl.num_programs(2) - 1)
    def _():
        cout_ref[0, 0, 0] = c_new
        nout_ref[0, 0] = n_new
        mout_ref[0, 0] = jnp.broadcast_to(m_new, (1, 128))


def mlstm_cell(q, k, kt, v, xc, up, gates, lw, nseq, nchunk, lc, valid, layer, nlayer, init, c_prev):
    hd = A_HEAD_DIM
    nt = nseq * nchunk
    g = gates[:, :2 * A_HEADS].reshape(nt, lc, 2 * A_HEADS)
    g_row = jnp.transpose(g, (0, 2, 1))[:, :, None, :]
    row_spec = pl.BlockSpec((lc, hd), lambda s, h, c: (s * nchunk + c, h))
    in_specs = [row_spec, row_spec,
                pl.BlockSpec((1, hd, lc), lambda s, h, c: (s * nchunk + c, h, 0)),
                row_spec, row_spec,
                pl.BlockSpec((lc, hd), lambda s, h, c: (s * nchunk + c, A_HEADS + h)),
                pl.BlockSpec((1, 1, 1, lc), lambda s, h, c: (s * nchunk + c, h, 0, 0)),
                pl.BlockSpec((1, 1, 1, lc), lambda s, h, c: (s * nchunk + c, A_HEADS + h, 0, 0)),
                pl.BlockSpec((1, hd), lambda s, h, c: (0, h)),
                pl.BlockSpec((1, hd), lambda s, h, c: (0, h))]
    args = [q, k, kt, v, xc, up, g_row, g_row, lw["mh_w"], lw["skip"]]
    has_init = init is not None
    if has_init:
        c0, n0, m0 = init
        in_specs += [pl.BlockSpec((1, 1, 1, hd, hd), lambda s, h, c: (layer, s, h, 0, 0)),
                     pl.BlockSpec((1, 1, 1, 1, hd), lambda s, h, c: (layer, s, h, 0, 0)),
                     pl.BlockSpec((1, 1, 1, 1, 1), lambda s, h, c: (layer, s, h, 0, 0))]
        args += [c0, n0.reshape(n0.shape[:3] + (1, hd)), m0.reshape(m0.shape + (1, 1))]
    has_prev = c_prev is not None
    aliases = {}
    if has_prev:
        aliases = {len(args): 1}
        in_specs.append(pl.BlockSpec(memory_space=pl.ANY))
        args.append(c_prev)
    out, c_out, n_out, m_out = pl.pallas_call(
        functools.partial(_mlstm_cell_kernel, lc=lc, valid=valid, has_init=has_init, has_prev=has_prev),
        grid=(nseq, A_HEADS, nchunk),
        in_specs=in_specs,
        out_specs=[row_spec,
                   pl.BlockSpec((1, 1, 1, hd, hd), lambda s, h, c: (layer, s, h, 0, 0)),
                   pl.BlockSpec((1, 1, 1, hd), lambda s, h, c: (s, h, 0, 0)),
                   pl.BlockSpec((1, 1, 1, 128), lambda s, h, c: (s, h, 0, 0))],
        out_shape=[jax.ShapeDtypeStruct((nt * lc, A_INNER), BF16),
                   jax.ShapeDtypeStruct((nlayer, nseq, A_HEADS, hd, hd), F32),
                   jax.ShapeDtypeStruct((nseq, A_HEADS, 1, hd), F32),
                   jax.ShapeDtypeStruct((nseq, A_HEADS, 1, 128), F32)],
        scratch_shapes=[pltpu.VMEM((hd, hd), F32), pltpu.VMEM((1, hd), F32), pltpu.VMEM((1, 1), F32)],
        input_output_aliases=aliases,
        compiler_params=_params(("parallel", "parallel", "arbitrary"), 40),
        name="mlstm_cell",
    )(*args)
    return out, c_out, n_out[:, :, 0, :], m_out[:, :, 0, 0]


def _mlstm_weights(layer, a_norm_pre, a_w_up, a_conv_w, a_conv_b, a_w_q, a_w_k, a_w_v, a_w_if, a_b_if,
                   a_mh_w, a_skip, a_w_down, a_norm_post):
    wif = a_w_if[layer].reshape(3, A_INNER, 2 * A_HEADS)
    wif = jnp.pad(wif, ((0, 0), (0, 0), (0, 128 - 2 * A_HEADS))).astype(BF16)
    return {
        "norm_pre": a_norm_pre[layer],
        "w_up": a_w_up[layer].astype(BF16),
        "conv_w8": jnp.pad(a_conv_w[layer], ((0, 8 - A_CONV), (0, 0))),
        "conv_b": a_conv_b[layer].reshape(1, A_INNER),
        "bd": jnp.stack([_headwise_tiles(a_w_q[layer]), _headwise_tiles(a_w_k[layer]),
                         _headwise_tiles(a_w_v[layer])]).astype(BF16),
        "wif": wif,
        "bif": jnp.pad(a_b_if[layer], (0, 128 - 2 * A_HEADS)).reshape(1, 128),
        "mh_w": a_mh_w[layer].reshape(1, A_INNER),
        "skip": a_skip[layer].reshape(1, A_INNER),
        "w_down": a_w_down[layer].astype(BF16),
        "norm_post": a_norm_post[layer],
    }


def mlstm_layer_prompt(x, lw, nseq, seq, layer, nlayer, c_prev):
    lc = CHUNK_P
    nchunk = seq // lc
    up = norm_matmul(x, lw["norm_pre"], lw["w_up"], TM_PROJ, 1024)
    q, k, v, xc, gates, kt = mlstm_pre(up, up, nseq, nchunk, lc, 1, lw, zero_first=True, emit_kt=True)
    out, c, n, m = mlstm_cell(q, k, kt, v, xc, up, gates, lw, nseq, nchunk, lc, lc, layer, nlayer, None, c_prev)
    x_new = matmul_norm_res(out, lw["w_down"], lw["norm_post"], x, 512)
    conv_new = up.reshape(nseq, seq, 2 * A_INNER)[:, seq - (A_CONV - 1):, :A_INNER]
    return x_new, c, n, m, conv_new


def mlstm_layer_sample(x, lw, nseq, ds, layer, state, conv0, c_prev):
    nsub = 32
    assert nseq % nsub == 0
    up = norm_matmul(x, lw["norm_pre"], lw["w_up"], TM_PROJ, 1024)
    halo = jnp.pad(conv0, ((0, 0), (8 - (A_CONV - 1), 0), (0, 0))).reshape(nseq * 8, A_INNER)
    q, k, v, xc, gates = mlstm_pre(up, halo, nseq // nsub, 1, SPAD, nsub, lw, zero_first=False, emit_kt=False)
    kt = jnp.transpose(k.reshape(nseq, SPAD, A_INNER), (0, 2, 1))
    out, c, n, m = mlstm_cell(q, k, kt, v, xc, up, gates, lw, nseq, 1, SPAD, ds, layer, state[0].shape[0],
                              state, c_prev)
    x_new = matmul_norm_res(out, lw["w_down"], lw["norm_post"], x, 512)
    xm = up.reshape(nseq, SPAD, 2 * A_INNER)[:, :ds, :A_INNER]
    conv_new = jnp.concatenate([conv0, xm], axis=1)[:, ds:]
    return x_new, c, n, m, conv_new


CMP_PACK = 4


def _compress_kernel(pt_ref, *refs, npages, feature_major):
    del pt_ref
    pages = refs[:npages]
    pos_ref, w1_ref, b1_ref, w2_ref, b2_ref, o_ref, stage_ref, acc_ref = refs[npages:]
    nblk = acc_ref.shape[1]
    ncb = stage_ref.shape[0]
    for p in range(npages):
        for cb in range(ncb):
            if feature_major:
                blk = pages[p][0, cb * 128:(cb + 1) * 128, :].T
            else:
                blk = pages[p][0, :, cb * 128:(cb + 1) * 128]
            stage_ref[cb, p * PAGE:(p + 1) * PAGE, :] = blk

    low = _low_half(nblk)
    for quad in range(CMP_STRIDE // CMP_PACK):
        for cb in range(ncb):
            xs = [stage_ref[cb, pl.ds(CMP_PACK * quad + i, nblk, stride=CMP_STRIDE), :] for i in range(CMP_PACK)]
            for gg in range(2):
                if gg == 0:
                    tiles = [jnp.where(low, xs[i], pltpu.roll(xs[i + 1], B_HEAD_DIM, axis=1))
                             for i in range(0, CMP_PACK, 2)]
                else:
                    tiles = [jnp.where(low, pltpu.roll(xs[i], B_HEAD_DIM, axis=1), xs[i + 1])
                             for i in range(0, CMP_PACK, 2)]
                packed = jnp.concatenate(tiles, axis=1)
                c = 2 * cb + gg
                slot = c // B_KV_HEADS
                for half in range(2):
                    idx = half * (CMP_STRIDE // CMP_PACK) + quad
                    xb = (packed + pos_ref[slot, idx]).astype(BF16)
                    d = jnp.dot(xb, w1_ref[slot, idx], preferred_element_type=F32)
                    if quad == 0:
                        acc_ref[2 * c + half] = d
                    else:
                        acc_ref[2 * c + half] += d

    out = jnp.zeros((nblk, 2 * B_KV_HEADS * B_HEAD_DIM), F32)
    for c in range(2 * B_KV_HEADS):
        slot = c // B_KV_HEADS
        nxt = pltpu.roll(acc_ref[2 * c + 1], nblk - 1, axis=0)
        hpre = acc_ref[2 * c] + nxt + b1_ref[slot]
        hid = hpre * _sigmoid(hpre)
        out = out + jnp.dot(hid.astype(BF16), w2_ref[c], preferred_element_type=F32)
    out = out + b2_ref[...]
    rid = lax.broadcasted_iota(jnp.int32, (nblk, 1), 0)
    o_ref[0] = jnp.where(rid < nblk - 1, out, 0.0)


def compress(pages_arr, table, nb, cw, feature_major):
    npages = table.shape[0] // nb
    nblk = npages * (PAGE // CMP_STRIDE)
    width = 2 * B_KV_HEADS * B_HEAD_DIM
    npk = CMP_LEN // CMP_PACK

    def page_map(b, pt, p):
        return (pt[b * npages + p], 0, 0)

    def const(shape):
        return pl.BlockSpec(shape, lambda b, pt: (0,) * len(shape))

    page_block = (1, width, PAGE) if feature_major else (1, PAGE, width)
    grid_spec = pltpu.PrefetchScalarGridSpec(
        num_scalar_prefetch=1,
        grid=(nb,),
        in_specs=[pl.BlockSpec(page_block, functools.partial(page_map, p=p)) for p in range(npages)]
        + [const((2, npk, 1, CMP_PACK * B_HEAD_DIM)), const((2, npk, CMP_PACK * B_HEAD_DIM, CMP_HIDDEN)),
           const((2, 1, CMP_HIDDEN)), const((2 * B_KV_HEADS, CMP_HIDDEN, width)), const((1, width))],
        out_specs=pl.BlockSpec((1, nblk, width), lambda b, pt: (b, 0, 0)),
        scratch_shapes=[pltpu.VMEM((width // 128, npages * PAGE, 128), F32),
                        pltpu.VMEM((4 * B_KV_HEADS, nblk, CMP_HIDDEN), F32)],
    )
    return pl.pallas_call(
        functools.partial(_compress_kernel, npages=npages, feature_major=feature_major),
        grid_spec=grid_spec,
        out_shape=jax.ShapeDtypeStruct((nb, nblk, width), F32),
        compiler_params=_params(("arbitrary",), 48),
        name="nsa_compress",
    )(table, *([pages_arr] * npages), cw["pos"], cw["w1"], cw["b1"], cw["w2p"], cw["b2p"])


def _compress_weights(cmp_pos, cmp_w1, cmp_b1, cmp_w2, cmp_b2):
    width = 2 * B_KV_HEADS * B_HEAD_DIM
    npk = CMP_LEN // CMP_PACK
    w2p = jnp.zeros((2 * B_KV_HEADS, CMP_HIDDEN, width), F32)
    for c in range(2 * B_KV_HEADS):
        w2p = w2p.at[c, :, c * B_HEAD_DIM:(c + 1) * B_HEAD_DIM].set(cmp_w2[c // B_KV_HEADS])
    return {
        "pos": cmp_pos.reshape(2, npk, 1, CMP_PACK * B_HEAD_DIM),
        "w1": cmp_w1.reshape(2, npk, CMP_PACK * B_HEAD_DIM, CMP_HIDDEN).astype(BF16),
        "b1": cmp_b1.reshape(2, 1, CMP_HIDDEN),
        "w2p": w2p.astype(BF16),
        "b2p": jnp.tile(cmp_b2[:, None, :], (1, B_KV_HEADS, 1)).reshape(1, width),
    }


def _softmax_masked(s, mask):
    s = jnp.where(mask, s, NEG)
    e = jnp.exp(s - jnp.max(s, axis=1, keepdims=True))
    return jnp.where(mask, e / jnp.sum(e, axis=1, keepdims=True), 0.0)


def _dot_nt(a, b):
    return lax.dot_general(a, b, (((1,), (1,)), ((), ())), preferred_element_type=F32)


def _low_half(tq):
    return lax.broadcasted_iota(jnp.int32, (tq, 128), 1) < B_HEAD_DIM


def _pad_heads(qblk, gl, tq):
    low = _low_half(tq)
    keep = low if gl == 0 else jnp.logical_not(low)
    tiles = []
    for i in range(4):
        hh = gl * 4 + i
        tile = qblk[:, (hh // 2) * 128:(hh // 2 + 1) * 128]
        if hh % 2 != gl:
            tile = pltpu.roll(tile, B_HEAD_DIM, axis=1)
        tiles.append(jnp.where(keep, tile, 0.0))
    return jnp.concatenate(tiles, axis=0).astype(BF16)


def _select_blocks(psum, qpos1, cover, ns):
    hi = psum.astype(BF16)
    lo = (psum - hi.astype(F32)).astype(BF16)
    imp = jnp.dot(hi, cover, preferred_element_type=F32) + jnp.dot(lo, cover, preferred_element_type=F32)
    s_idx = lax.broadcasted_iota(jnp.int32, (1, 128), 1)
    cur = jnp.right_shift(qpos1, 6)
    assert SEL_LEN == 64
    valid = s_idx * SEL_LEN <= qpos1
    forced = (s_idx == 0) | (s_idx == cur) | (s_idx == cur - 1)
    score = jnp.where(valid, imp + jnp.where(forced, FORCE_BONUS, 0.0), NEG)
    score = jnp.where(s_idx < ns, score, 2 * NEG)
    rank = jnp.zeros_like(score)
    for sp in range(ns):
        colv = score[:, sp:sp + 1]
        ahead = (colv > score) | ((colv == score) & (s_idx > sp))
        rank = rank + jnp.where(ahead, 1.0, 0.0)
    return jnp.where((rank < min(SEL_TOPN, ns)) & (s_idx < ns), 1.0, 0.0)


def _select_blocks_t(psum, qpos_row, cover_t, ns):
    tq = psum.shape[0]
    pt = psum.T
    hi = pt.astype(BF16)
    lo = (pt - hi.astype(F32)).astype(BF16)
    imp = jnp.dot(cover_t, hi, preferred_element_type=F32) + jnp.dot(cover_t, lo, preferred_element_type=F32)
    nr = -(-ns // 8) * 8
    imp = imp[0:nr]
    s_idx = lax.broadcasted_iota(jnp.int32, (nr, 1), 0)
    cur = jnp.right_shift(qpos_row, 6)
    valid = s_idx * SEL_LEN <= qpos_row
    forced = (s_idx == 0) | (s_idx == cur) | (s_idx == cur - 1)
    score = jnp.where(valid, imp + jnp.where(forced, FORCE_BONUS, 0.0), NEG)
    score = jnp.where(s_idx < ns, score, 2 * NEG)
    rank = jnp.zeros_like(score)
    for sp in range(ns):
        rowv = score[sp:sp + 1, :]
        ahead = (rowv > score) | ((rowv == score) & (s_idx > sp))
        rank = rank + jnp.where(ahead, 1.0, 0.0)
    sel_t = jnp.where((rank < min(SEL_TOPN, ns)) & (s_idx < ns), 1.0, 0.0)
    return jnp.concatenate([sel_t, jnp.zeros((128 - nr, tq), F32)], axis=0).T


def _cmp_branch(qs, tq, qpos, kc, vc):
    n_idx = lax.broadcasted_iota(jnp.int32, (1, 128), 1)
    p_c = _softmax_masked(_dot_nt(qs, kc), n_idx * CMP_STRIDE + (CMP_LEN - 1) <= qpos)
    o_c = jnp.dot(p_c.astype(BF16), vc, preferred_element_type=F32)
    return o_c, p_c[0:tq] + p_c[tq:2 * tq] + p_c[2 * tq:3 * tq] + p_c[3 * tq:4 * tq]


def _softmax_pv(s, tq, bias, pv):
    nk = s.shape[1]
    s3 = s.reshape(4, tq, nk) + bias[None]
    e = jnp.exp(s3 - jnp.max(s3, axis=2, keepdims=True))
    den = jnp.sum(e, axis=2, keepdims=True)
    o = pv(e.reshape(4 * tq, nk).astype(BF16))
    return o * (1.0 / den.reshape(4 * tq, 1))


def _combine(o_heads, sg, zs, tq):
    low = _low_half(tq)
    y = jnp.zeros((tq, 512), F32)
    for br in range(3):
        tiles = []
        for j in range(4):
            c = br * 8 + 2 * j
            a = o_heads[br][2 * j] * sg[:, c:c + 1]
            b = o_heads[br][2 * j + 1] * sg[:, c + 1:c + 2]
            if j // 2 == 0:
                tiles.append(jnp.where(low, a, pltpu.roll(b, B_HEAD_DIM, axis=1)))
            else:
                tiles.append(jnp.where(low, pltpu.roll(a, B_HEAD_DIM, axis=1), b))
        z = zs[br].astype(F32)
        y = y + jnp.concatenate(tiles, axis=1) * (z * _sigmoid(z))
    return y


def _nsa_prompt_kernel(q_ref, z0_ref, z1_ref, z2_ref, gate_ref, kc_ref, vc_ref, ks_ref, vs_ref, kw_ref, vw_ref,
                       cover_t_ref, blockneg_ref, y_ref, osel_sc, kaug_sc, *, tq, seq, kstep):
    q0 = pl.program_id(2) * tq
    qblk = q_ref[...].astype(F32) * (B_HEAD_DIM ** -0.5)
    qpos1 = q0 + lax.broadcasted_iota(jnp.int32, (tq, 1), 0)
    qpos_row = q0 + lax.broadcasted_iota(jnp.int32, (1, tq), 1)
    qpos = jnp.concatenate([qpos1] * 4, axis=0)
    wlen = WINDOW + tq
    wstart = pl.multiple_of(jnp.maximum(q0 - WINDOW, 0), 128)
    kw = kw_ref[pl.ds(wstart, wlen), :]
    vw = vw_ref[pl.ds(wstart, wlen), :]
    kpos_w = wstart + lax.broadcasted_iota(jnp.int32, (1, wlen), 1)
    wbias = jnp.where((kpos_w <= qpos1) & (kpos_w > qpos1 - WINDOW), 0.0, NEG)
    kc = kc_ref[0].astype(BF16)
    vc = vc_ref[0].astype(BF16)

    @pl.when(pl.program_id(2) == 0)
    def _():
        kaug_sc[:, 0:128] = ks_ref[...]
        kaug_sc[:, 128:256] = blockneg_ref[...]

    nbefore = (q0 + kstep - 1) // kstep
    first_blk = q0 // SEL_LEN
    kd = ks_ref[pl.ds(pl.multiple_of(q0, 128), tq), :]
    vd = vs_ref[pl.ds(pl.multiple_of(q0, 128), tq), :]
    tri = jnp.where(lax.broadcasted_iota(jnp.int32, (tq, tq), 1) <= lax.broadcasted_iota(jnp.int32, (tq, tq), 0),
                    0.0, NEG)
    s_idx = lax.broadcasted_iota(jnp.int32, (1, 128), 1)
    o_heads = [[None] * 8 for _ in range(3)]
    for gl in range(2):
        qs = _pad_heads(qblk, gl, tq)
        o_c, psum = _cmp_branch(qs, tq, qpos, kc, vc)
        sel = _select_blocks_t(psum, qpos_row, cover_t_ref[...], seq // SEL_LEN)
        skip = jnp.where((sel > 0.5) & (s_idx < first_blk), 0.0, 1.0).astype(BF16)
        q_aug = jnp.concatenate([qs, jnp.concatenate([skip] * 4, axis=0)], axis=1)
        s_d = _dot_nt(qs, kd).reshape(4, tq, tq) + tri[None]
        m_d = jnp.max(s_d, axis=2, keepdims=True)

        @pl.when(nbefore == 0)
        def _(gl=gl, s_d=s_d, m_d=m_d):
            e_d = jnp.exp(s_d - m_d)
            den = jnp.sum(e_d, axis=2, keepdims=True)
            o = jnp.dot(e_d.reshape(4 * tq, tq).astype(BF16), vd, preferred_element_type=F32)
            osel_sc[gl] = o * (1.0 / den.reshape(4 * tq, 1))

        for v in range(1, seq // kstep + 1):
            @pl.when(nbefore == v)
            def _(nk=v * kstep, gl=gl, q_aug=q_aug, s_d=s_d, m_d=m_d):
                s_b = _dot_nt(q_aug, kaug_sc[0:nk, :]).reshape(4, tq, nk)
                m = jnp.maximum(m_d, jnp.max(s_b, axis=2, keepdims=True))
                e_b = jnp.exp(s_b - m)
                e_d = jnp.exp(s_d - m)
                den = jnp.sum(e_b, axis=2, keepdims=True) + jnp.sum(e_d, axis=2, keepdims=True)
                o = (jnp.dot(e_b.reshape(4 * tq, nk).astype(BF16), vs_ref[0:nk, :], preferred_element_type=F32)
                     + jnp.dot(e_d.reshape(4 * tq, tq).astype(BF16), vd, preferred_element_type=F32))
                osel_sc[gl] = o * (1.0 / den.reshape(4 * tq, 1))

        o_w = _softmax_pv(_dot_nt(qs, kw), tq, wbias, lambda e: jnp.dot(e, vw, preferred_element_type=F32))
        outs = (o_c, osel_sc[gl], o_w)
        for br in range(3):
            for i in range(4):
                o_heads[br][gl * 4 + i] = outs[br][i * tq:(i + 1) * tq]
    y = _combine(o_heads, _sigmoid(gate_ref[...]), (z0_ref[...], z1_ref[...], z2_ref[...]), tq)
    y_ref[...] = y.astype(BF16)


def _cover_matrix(transposed=False):
    n = np.arange(128)[:, None]
    s = np.arange(128)[None, :]
    cov = (n * CMP_STRIDE < s * SEL_LEN + SEL_LEN) & (n * CMP_STRIDE + CMP_LEN - 1 >= s * SEL_LEN)
    return jnp.asarray(cov.T if transposed else cov, BF16)


def _expand_matrix(nkeys):
    s = np.arange(128)[:, None]
    key = np.arange(nkeys)[None, :]
    return jnp.asarray(key // SEL_LEN == s, BF16)


def nsa_attend_prompt(pm, pg, cmp, kvb, nb, seq):
    tq = 128
    nq = seq // tq

    def rows(width, col):
        return pl.BlockSpec((tq, width), lambda b, gp, t: (b * nq + t, col(gp)))

    def seq_block(first):
        return pl.BlockSpec((seq, 128), lambda b, gp, t: (b, first + gp))

    kstep = 256
    assert seq % kstep == 0 and tq % 128 == 0
    return pl.pallas_call(
        functools.partial(_nsa_prompt_kernel, tq=tq, seq=seq, kstep=kstep),
        grid=(nb, 2, nq),
        in_specs=[rows(512, lambda gp: gp), rows(512, lambda gp: 2 + gp), rows(512, lambda gp: 4 + gp),
                  rows(512, lambda gp: 6 + gp), rows(128, lambda gp: gp),
                  pl.BlockSpec((1, 128, 128), lambda b, gp, t: (b, 0, gp)),
                  pl.BlockSpec((1, 128, 128), lambda b, gp, t: (b, 0, 2 + gp)),
                  seq_block(4), seq_block(6), seq_block(8), seq_block(10),
                  pl.BlockSpec((128, 128), lambda b, gp, t: (0, 0)),
                  pl.BlockSpec((seq, 128), lambda b, gp, t: (0, 0))],
        out_specs=rows(512, lambda gp: gp),
        out_shape=jax.ShapeDtypeStruct((nb * seq, B_ATT), BF16),
        scratch_shapes=[pltpu.VMEM((2, 4 * tq, 128), F32), pltpu.VMEM((seq, 256), BF16)],
        compiler_params=_params(("parallel", "parallel", "arbitrary"), 56),
        name="nsa_attend_prompt",
    )(pm, pm, pm, pm, pg, cmp, cmp, kvb, kvb, kvb, kvb, _cover_matrix(transposed=True),
      NEG * _expand_matrix(seq).T)


def _nsa_sample_kernel(pt_ref, *refs, npages, past):
    del pt_ref
    pages = refs[:npages]
    (q_ref, z0_ref, z1_ref, z2_ref, gate_ref, cmp_ref, new_ref, win_ref, cover_ref, expand_ref, y_ref,
     kst_sc, vst_sc) = refs[npages:]
    tq = SPAD
    ngp = 2
    rows_gp = 8 * tq
    wrows = win_ref.shape[2]
    ns = past // SEL_LEN + 1
    sg = _sigmoid(gate_ref[...])

    def qpos_rows(n):
        return past + jnp.bitwise_and(lax.broadcasted_iota(jnp.int32, (n, 1), 0), tq - 1)

    def new_rows(col):
        blk = new_ref[:, col:col + 128].astype(BF16)
        return jnp.concatenate([blk, jnp.zeros((NEW_PAD - tq, 128), BF16)], axis=0)

    def per_pair(fn):
        return jnp.concatenate([fn(gp) for gp in range(ngp)], axis=0)

    for gp in range(ngp):
        lo, hi = gp * 128, (gp + 1) * 128
        for p in range(npages):
            kst_sc[gp, :, p * PAGE:(p + 1) * PAGE] = pages[p][0, lo:hi, :].astype(BF16)
            vst_sc[gp, :, p * PAGE:(p + 1) * PAGE] = pages[p][0, 256 + lo:256 + hi, :].astype(BF16)
    qs = []
    for gp in range(ngp):
        qblk = q_ref[:, gp * 512:(gp + 1) * 512].astype(F32) * (B_HEAD_DIM ** -0.5)
        qs.append(jnp.concatenate([_pad_heads(qblk, 0, tq), _pad_heads(qblk, 1, tq)], axis=0))

    n_idx = lax.broadcasted_iota(jnp.int32, (1, 128), 1)
    s = per_pair(lambda gp: _dot_nt(qs[gp], cmp_ref[0, :, gp * 128:(gp + 1) * 128].astype(BF16)))
    p_c = _softmax_masked(s, n_idx * CMP_STRIDE + (CMP_LEN - 1) <= qpos_rows(ngp * rows_gp))
    p_cb = p_c.astype(BF16)
    o_c = [jnp.dot(p_cb[gp * rows_gp:(gp + 1) * rows_gp],
                   cmp_ref[0, :, 256 + gp * 128:256 + (gp + 1) * 128].astype(BF16), preferred_element_type=F32)
           for gp in range(ngp)]
    psum = jnp.concatenate([sum(p_c[(g * 4 + i) * tq:(g * 4 + i + 1) * tq] for i in range(4))
                            for g in range(2 * ngp)], axis=0)
    nsel_rows = 2 * ngp * tq
    sel = _select_blocks(psum, qpos_rows(nsel_rows), cover_ref[...], ns).astype(BF16)
    selk = jnp.dot(sel, expand_ref[...], preferred_element_type=F32) > 0.5
    kpos_s = lax.broadcasted_iota(jnp.int32, (1, past + NEW_PAD), 1)
    sbias = jnp.where(selk & (kpos_s <= qpos_rows(nsel_rows)), 0.0, NEG)
    sbias = jnp.concatenate([sbias[g * tq:(g + 1) * tq] for g in range(2 * ngp) for _ in range(4)], axis=0)
    s = per_pair(lambda gp: jnp.concatenate(
        [jnp.dot(qs[gp], kst_sc[gp], preferred_element_type=F32),
         _dot_nt(qs[gp], new_rows(512 + gp * 128))], axis=1)) + sbias
    e = jnp.exp(s - jnp.max(s, axis=1, keepdims=True))
    rden = 1.0 / jnp.sum(e, axis=1, keepdims=True)
    e = e.astype(BF16)
    o_s = [(_dot_nt(e[gp * rows_gp:(gp + 1) * rows_gp, 0:past], vst_sc[gp])
            + jnp.dot(e[gp * rows_gp:(gp + 1) * rows_gp, past:], new_rows(768 + gp * 128),
                      preferred_element_type=F32)) * rden[gp * rows_gp:(gp + 1) * rows_gp]
           for gp in range(ngp)]
    kpos_w = (past - wrows) + lax.broadcasted_iota(jnp.int32, (1, wrows + NEW_PAD), 1)
    qpos = qpos_rows(ngp * rows_gp)
    wbias = jnp.where((kpos_w <= qpos) & (kpos_w > qpos - WINDOW) & (kpos_w >= 0), 0.0, NEG)
    s = per_pair(lambda gp: jnp.concatenate(
        [jnp.dot(qs[gp], win_ref[0, gp * 128:(gp + 1) * 128, :].astype(BF16), preferred_element_type=F32),
         _dot_nt(qs[gp], new_rows(1024 + gp * 128))], axis=1)) + wbias
    e = jnp.exp(s - jnp.max(s, axis=1, keepdims=True))
    rden = 1.0 / jnp.sum(e, axis=1, keepdims=True)
    e = e.astype(BF16)
    o_w = [(_dot_nt(e[gp * rows_gp:(gp + 1) * rows_gp, 0:wrows],
                    win_ref[0, 256 + gp * 128:256 + (gp + 1) * 128, :].astype(BF16))
            + jnp.dot(e[gp * rows_gp:(gp + 1) * rows_gp, wrows:], new_rows(1280 + gp * 128),
                      preferred_element_type=F32)) * rden[gp * rows_gp:(gp + 1) * rows_gp]
           for gp in range(ngp)]

    for gp in range(ngp):
        o_heads = [[outs[gp][hh * tq:(hh + 1) * tq] for hh in range(8)] for outs in (o_c, o_s, o_w)]
        zs = (z0_ref[:, gp * 512:(gp + 1) * 512], z1_ref[:, gp * 512:(gp + 1) * 512],
              z2_ref[:, gp * 512:(gp + 1) * 512])
        y_ref[:, gp * 512:(gp + 1) * 512] = _combine(o_heads, sg[:, gp * 128:(gp + 1) * 128], zs, tq).astype(BF16)


def nsa_attend_sample(pm, pg, cmp, kv_new, cache_pages, table, cache_win, nb, past):
    npages = past // PAGE
    wrows = cache_win.shape[2]
    nks = past + NEW_PAD
    assert past % SEL_LEN == 0 and SPAD <= SEL_LEN

    def page_map(b, pt, p):
        return (pt[b * npages + p], 1, 0)

    def rows(width, col):
        return pl.BlockSpec((SPAD, width), lambda b, pt: (b, col))

    grid_spec = pltpu.PrefetchScalarGridSpec(
        num_scalar_prefetch=1,
        grid=(nb,),
        in_specs=[pl.BlockSpec((1, 512, PAGE), functools.partial(page_map, p=p)) for p in range(npages)]
        + [rows(B_ATT, 0), rows(B_ATT, 1), rows(B_ATT, 2), rows(B_ATT, 3), rows(256, 0),
           pl.BlockSpec((1, 128, 512), lambda b, pt: (b, 0, 0)),
           rows(1536, 0),
           pl.BlockSpec((1, 512, wrows), lambda b, pt: (b, 0, 0)),
           pl.BlockSpec((128, 128), lambda b, pt: (0, 0)),
           pl.BlockSpec((128, nks), lambda b, pt: (0, 0))],
        out_specs=rows(B_ATT, 0),
        scratch_shapes=[pltpu.VMEM((2, 128, past), BF16), pltpu.VMEM((2, 128, past), BF16)],
    )
    return pl.pallas_call(
        functools.partial(_nsa_sample_kernel, npages=npages, past=past),
        grid_spec=grid_spec,
        out_shape=jax.ShapeDtypeStruct((nb * SPAD, B_ATT), BF16),
        compiler_params=_params(("arbitrary",), 48),
        name="nsa_attend_sample",
    )(table, *([cache_pages] * npages), pm, pm, pm, pm, pg, cmp, kv_new, cache_win,
      _cover_matrix(), _expand_matrix(nks))


def _nsa_weights(lb, b_norm_pre, b_w_in, b_w_out, b_norm_post):
    w_in = b_w_in[lb]
    wg = w_in[:, 4 * B_ATT:].reshape(D_MODEL, 3, 2, 8)
    wg = jnp.transpose(wg, (0, 2, 1, 3)).reshape(D_MODEL, 2, 24)
    wg = jnp.pad(wg, ((0, 0), (0, 0), (0, 128 - 24))).reshape(D_MODEL, 256)
    return {"norm_pre": b_norm_pre[lb], "w_main": w_in[:, :4 * B_ATT].astype(BF16), "w_gate": wg.astype(BF16),
            "w_out": b_w_out[lb].astype(BF16), "norm_post": b_norm_post[lb]}


def kernel(x_prompt, x_sample, cache_kv, page_table, cache_win, state_C, state_n, state_m, state_conv,
           a_norm_pre, a_w_up, a_conv_w, a_conv_b, a_w_q, a_w_k, a_w_v, a_w_if, a_b_if,
           a_mh_w, a_skip, a_w_down, a_norm_post,
           kv_norm, w_kv, cmp_pos, cmp_w1, cmp_b1, cmp_w2, cmp_b2,
           b_norm_pre, b_w_in, b_w_out, b_norm_post):
    bp, sp = x_prompt.shape[0], x_prompt.shape[1]
    bs, ds = x_sample.shape[0], x_sample.shape[1]
    n_a = state_C.shape[0]
    n_b = b_w_in.shape[0]
    past = page_table.shape[1] * cache_kv.shape[1]
    assert cache_kv.shape[1] == PAGE and sp % PAGE == 0 and ds <= SPAD
    assert (past + ds - CMP_LEN) // CMP_STRIDE + 1 == past // CMP_STRIDE - 1
    xp = x_prompt.reshape(bp * sp, D_MODEL)
    xs = jnp.pad(x_sample, ((0, 0), (0, SPAD - ds), (0, 0))).reshape(bs * SPAD, D_MODEL)

    np_l, ns_l, mp_l, ms_l, vp_l, vs_l = [], [], [], [], [], []
    c_prompt = c_sample = None
    for layer in range(n_a):
        lw = _mlstm_weights(layer, a_norm_pre, a_w_up, a_conv_w, a_conv_b, a_w_q, a_w_k, a_w_v, a_w_if, a_b_if,
                            a_mh_w, a_skip, a_w_down, a_norm_post)
        xp, c_prompt, n1, m1, v1 = mlstm_layer_prompt(xp, lw, bp, sp, layer, n_a, c_prompt)
        xs, c_sample, n2, m2, v2 = mlstm_layer_sample(xs, lw, bs, ds, layer, (state_C, state_n, state_m),
                                                      state_conv[layer], c_sample)
        np_l.append(n1); mp_l.append(m1); vp_l.append(v1)
        ns_l.append(n2); ms_l.append(m2); vs_l.append(v2)

    nkv = w_kv.shape[1]
    w_kv_b = w_kv.astype(BF16)
    kvp, kvp_b = norm_matmul(xp, kv_norm, w_kv_b, TM_PROJ, 512, also_bf16=True)
    kvs = norm_matmul(xs, kv_norm, w_kv_b, TM_PROJ, 512)
    cw = _compress_weights(cmp_pos, cmp_w1, cmp_b1, cmp_w2, cmp_b2)
    table_p = jnp.arange(bp * (sp // PAGE), dtype=jnp.int32)
    table_s = page_table.reshape(-1).astype(jnp.int32)
    cache_pages = jnp.transpose(cache_kv, (0, 2, 3, 4, 1)).reshape(cache_kv.shape[0], -1, PAGE)
    cmp_p = compress(kvp.reshape(bp * sp // PAGE, PAGE, nkv), table_p, bp, cw, feature_major=False)
    cmp_s = compress(cache_pages, table_s, bs, cw, feature_major=True)
    wb = cache_win.shape[1]
    win2 = jnp.transpose(cache_win, (0, 2, 3, 4, 1)).reshape(bs, -1, wb)

    for lb in range(n_b):
        bw = _nsa_weights(lb, b_norm_pre, b_w_in, b_w_out, b_norm_post)
        pm = norm_matmul(xp, bw["norm_pre"], bw["w_main"], TM_PROJ, 1024, out_dtype=BF16)
        pg = norm_matmul(xp, bw["norm_pre"], bw["w_gate"], TM_PROJ, 256)
        yp = nsa_attend_prompt(pm, pg, cmp_p, kvp_b, bp, sp)
        xp = matmul_norm_res(yp, bw["w_out"], bw["norm_post"], xp, 512)
        pm = norm_matmul(xs, bw["norm_pre"], bw["w_main"], TM_PROJ, 1024, out_dtype=BF16)
        pg = norm_matmul(xs, bw["norm_pre"], bw["w_gate"], TM_PROJ, 256)
        ys = nsa_attend_sample(pm, pg, cmp_s, kvs, cache_pages, table_s, win2, bs, past)
        xs = matmul_norm_res(ys, bw["w_out"], bw["norm_post"], xs, 512)

    y_prompt = xp.reshape(bp, sp, D_MODEL)
    y_sample = xs.reshape(bs, SPAD, D_MODEL)[:, :ds]
    kvp3 = kvp.reshape(bp, sp, nkv)
    kvs3 = kvs.reshape(bs, SPAD, nkv)[:, :ds]
    ncache = 4 * B_KV_HEADS * B_HEAD_DIM
    kv_rows_prompt = kvp3[:, :, :ncache].reshape(bp, sp, 4, B_KV_HEADS, B_HEAD_DIM)
    kv_rows_sample = kvs3[:, :, :ncache].reshape(bs, ds, 4, B_KV_HEADS, B_HEAD_DIM)
    win_prompt = kvp3[:, sp - min(WINDOW, sp):, ncache:].reshape(bp, min(WINDOW, sp), 2, B_KV_HEADS, B_HEAD_DIM)
    win_new = kvs3[:, :, ncache:].reshape(bs, ds, 2, B_KV_HEADS, B_HEAD_DIM).astype(cache_win.dtype)
    win_sample = jnp.concatenate([cache_win, win_new], axis=1)[:, ds:]
    return (y_prompt, y_sample, kv_rows_prompt, kv_rows_sample, win_prompt, win_sample,
            c_prompt, c_sample, jnp.stack(np_l), jnp.stack(ns_l),
            jnp.stack(mp_l), jnp.stack(ms_l), jnp.stack(vp_l), jnp.stack(vs_l))
```

```python
import functools

import jax
import jax.numpy as jnp
import numpy as np
from jax import lax
from jax.experimental import pallas as pl
from jax.experimental.pallas import tpu as pltpu

F32 = jnp.float32
BF16 = jnp.bfloat16

D_MODEL = 1024
A_INNER = 2048
A_HEADS = 4
A_HEAD_DIM = 512
A_CONV = 4
B_HEADS = 16
B_HEAD_DIM = 64
B_KV_HEADS = 4
B_ATT = 1024
CMP_LEN = 32
CMP_STRIDE = 16
CMP_HIDDEN = 256
SEL_LEN = 64
SEL_TOPN = 16
WINDOW = 512
PAGE = 128
FORCE_BONUS = 1.0e3
NEG = -1.0e30
EPS = 1e-6

SPAD = 16
NEW_PAD = 128
CHUNK_P = 512
TM_PROJ = 1024
V7X_VMEM_BYTES = 64 * 2**20


def _params(sem, vmem_mb):
    assert vmem_mb * 2**20 < V7X_VMEM_BYTES
    return pltpu.CompilerParams(dimension_semantics=sem, vmem_limit_bytes=vmem_mb * 2**20)


def _sigmoid(x):
    return 1.0 / (1.0 + jnp.exp(-x))


def _norm_matmul_kernel(x_ref, nw_ref, w_ref, *rest):
    o_refs, xn_ref = rest[:-1], rest[-1]

    @pl.when(pl.program_id(1) == 0)
    def _():
        x = x_ref[...]
        ms = jnp.mean(x * x, axis=-1, keepdims=True)
        xn_ref[...] = (x * lax.rsqrt(ms + EPS) * nw_ref[...]).astype(BF16)

    y = jnp.dot(xn_ref[...], w_ref[...], preferred_element_type=F32)
    for o_ref in o_refs:
        o_ref[...] = y.astype(o_ref.dtype)


def norm_matmul(x, nw, w, tm, tn, also_bf16=False, out_dtype=F32):
    m, k = x.shape
    n = w.shape[1]
    assert m % tm == 0 and n % tn == 0
    o_spec = pl.BlockSpec((tm, tn), lambda i, j: (i, j))
    out_specs = [o_spec]
    out_shape = [jax.ShapeDtypeStruct((m, n), out_dtype)]
    if also_bf16:
        out_specs.append(o_spec)
        out_shape.append(jax.ShapeDtypeStruct((m, n), BF16))
    res = pl.pallas_call(
        _norm_matmul_kernel,
        grid=(m // tm, n // tn),
        in_specs=[pl.BlockSpec((tm, k), lambda i, j: (i, 0)),
                  pl.BlockSpec((1, k), lambda i, j: (0, 0)),
                  pl.BlockSpec((k, tn), lambda i, j: (0, j))],
        out_specs=out_specs,
        out_shape=out_shape,
        scratch_shapes=[pltpu.VMEM((tm, k), BF16)],
        compiler_params=_params(("parallel", "arbitrary"), 40),
        name="norm_matmul",
    )(x, nw.reshape(1, k), w)
    return res if also_bf16 else res[0]


def _matmul_norm_res_kernel(a_ref, w_ref, nw_ref, x_ref, o_ref):
    y = jnp.dot(a_ref[...], w_ref[...], preferred_element_type=F32)
    ms = jnp.mean(y * y, axis=-1, keepdims=True)
    o_ref[...] = x_ref[...] + y * lax.rsqrt(ms + EPS) * nw_ref[...]


def matmul_norm_res(a, w, nw, x, tm):
    m, k = a.shape
    n = w.shape[1]
    assert m % tm == 0
    return pl.pallas_call(
        _matmul_norm_res_kernel,
        grid=(m // tm,),
        in_specs=[pl.BlockSpec((tm, k), lambda i: (i, 0)),
                  pl.BlockSpec((k, n), lambda i: (0, 0)),
                  pl.BlockSpec((1, n), lambda i: (0, 0)),
                  pl.BlockSpec((tm, n), lambda i: (i, 0))],
        out_specs=pl.BlockSpec((tm, n), lambda i: (i, 0)),
        out_shape=jax.ShapeDtypeStruct((m, n), F32),
        compiler_params=_params(("parallel",), 40),
        name="matmul_norm_res",
    )(a, w, nw.reshape(1, n), x)


def _mlstm_pre_kernel(xm_ref, prev_ref, cw_ref, cb_ref, bd_ref, wif_ref, bif_ref,
                      q_ref, k_ref, v_ref, xc_ref, g_ref, *rest, lsub, nsub, zero_first, emit_kt):
    if emit_kt:
        kt_ref, cat_ref = rest
    else:
        (cat_ref,) = rest
    t = pl.program_id(1)
    h = pl.program_id(2)
    x = xm_ref[...]
    prev = prev_ref[...]
    if zero_first:
        prev = jnp.where(t == 0, 0.0, prev)
    cw = cw_ref[...]
    pieces = []
    for s in range(nsub):
        base = s * (lsub + 8)
        xs = x[s * lsub:(s + 1) * lsub]
        cat_ref[base:base + 8, :] = prev[s * 8:(s + 1) * 8]
        cat_ref[base + 8:base + 8 + lsub, :] = xs
        acc = cb_ref[...] + xs * cw[A_CONV - 1:A_CONV]
        for j in range(1, A_CONV):
            acc = acc + cat_ref[base + 8 - j:base + 8 - j + lsub, :] * cw[A_CONV - 1 - j:A_CONV - j]
        pieces.append(acc)
    xconv = pieces[0] if nsub == 1 else jnp.concatenate(pieces, axis=0)
    xc = xconv * _sigmoid(xconv)

    xcb = xc.astype(BF16)
    xb = x.astype(BF16)
    ntile = x.shape[1] // 128

    def headwise(src, which):
        return jnp.concatenate(
            [jnp.dot(src[:, j * 128:(j + 1) * 128], bd_ref[which, j], preferred_element_type=F32)
             for j in range(ntile)], axis=1)

    q = headwise(xcb, 0)
    k = headwise(xcb, 1)
    v = headwise(xb, 2)

    g = (jnp.dot(q.astype(BF16), wif_ref[0], preferred_element_type=F32)
         + jnp.dot(k.astype(BF16), wif_ref[1], preferred_element_type=F32)
         + jnp.dot(v.astype(BF16), wif_ref[2], preferred_element_type=F32))

    @pl.when(h == 0)
    def _():
        g_ref[...] = bif_ref[...] + g

    @pl.when(h != 0)
    def _():
        g_ref[...] += g

    ks = k * (A_HEAD_DIM ** -0.5)
    q_ref[...] = q.astype(BF16)
    k_ref[...] = ks.astype(BF16)
    v_ref[...] = v.astype(BF16)
    xc_ref[...] = xc.astype(BF16)
    if emit_kt:
        kt_ref[0] = ks.T.astype(BF16)


def _headwise_tiles(w):
    n, c, _ = w.shape
    per = 128 // c
    wt = w.reshape(n // per, per, c, c)
    eye = jnp.eye(per, dtype=w.dtype)
    return jnp.einsum("nm,tncd->tncmd", eye, wt).reshape(n // per, 128, 128)


def mlstm_pre(up, halo, nseq, ntile, lsub, nsub, lw, zero_first, emit_kt):
    lt = nsub * lsub
    rows = nseq * ntile * lt
    hd = A_HEAD_DIM
    if zero_first:
        assert nsub == 1

        def prev_map(s, t, h):
            return (jnp.maximum((s * ntile + t) * (lt // 8) - 1, 0), h)
    else:
        assert ntile == 1

        def prev_map(s, t, h):
            return (s, h)
    wspec = pl.BlockSpec((8, hd), lambda s, t, h: (0, h))
    row_spec = pl.BlockSpec((lt, hd), lambda s, t, h: (s * ntile + t, h))
    out_specs = [row_spec, row_spec, row_spec, row_spec,
                 pl.BlockSpec((lt, 128), lambda s, t, h: (s * ntile + t, 0))]
    out_shape = [jax.ShapeDtypeStruct((rows, A_INNER), BF16)] * 4 + [jax.ShapeDtypeStruct((rows, 128), F32)]
    if emit_kt:
        out_specs.append(pl.BlockSpec((1, hd, lt), lambda s, t, h: (s * ntile + t, h, 0)))
        out_shape.append(jax.ShapeDtypeStruct((nseq * ntile, A_INNER, lt), BF16))
    return pl.pallas_call(
        functools.partial(_mlstm_pre_kernel, lsub=lsub, nsub=nsub, zero_first=zero_first, emit_kt=emit_kt),
        grid=(nseq, ntile, A_HEADS),
        in_specs=[row_spec,
                  pl.BlockSpec((8 * nsub, hd), prev_map),
                  wspec,
                  pl.BlockSpec((1, hd), lambda s, t, h: (0, h)),
                  pl.BlockSpec((3, hd // 128, 128, 128), lambda s, t, h: (0, h, 0, 0)),
                  pl.BlockSpec((3, hd, 128), lambda s, t, h: (0, h, 0)),
                  pl.BlockSpec((1, 128), lambda s, t, h: (0, 0))],
        out_specs=out_specs,
        out_shape=out_shape,
        scratch_shapes=[pltpu.VMEM((nsub * (lsub + 8), hd), F32)],
        compiler_params=_params(("parallel", "parallel", "arbitrary"), 40),
        name="mlstm_pre",
    )(up, halo, lw["conv_w8"], lw["conv_b"], lw["bd"], lw["wif"], lw["bif"])


def _mlstm_cell_kernel(*refs, lc, valid, has_init, has_prev):
    (q_ref, k_ref, kt_ref, v_ref, xc_ref, z_ref, igr_ref, fgr_ref, mh_ref, skip_ref) = refs[:10]
    rest = refs[10:]
    if has_init:
        c0_ref, n0_ref, m0_ref = rest[:3]
        rest = rest[3:]
    if has_prev:
        rest = rest[1:]
    o_ref, cout_ref, nout_ref, mout_ref, c_sc, n_sc, m_sc = rest
    c = pl.program_id(2)

    @pl.when(c == 0)
    def _():
        if has_init:
            c_sc[...] = c0_ref[0, 0, 0]
            n_sc[...] = n0_ref[0, 0, 0]
            m_sc[...] = m0_ref[0, 0, 0]
        else:
            c_sc[...] = jnp.zeros_like(c_sc)
            n_sc[...] = jnp.zeros_like(n_sc)
            m_sc[...] = jnp.full_like(m_sc, NEG)

    def logsig(x):
        return jnp.minimum(x, 0.0) - jnp.log(1.0 + jnp.exp(-jnp.abs(x)))

    ig_r = igr_ref[0, 0]
    lf_r = logsig(fgr_ref[0, 0])
    row = lax.broadcasted_iota(jnp.int32, (lc, lc), 0)
    col = lax.broadcasted_iota(jnp.int32, (lc, lc), 1)
    if valid < lc:
        cid = lax.broadcasted_iota(jnp.int32, (1, lc), 1)
        ig_r = jnp.where(cid < valid, ig_r, NEG)
        lf_r = jnp.where(cid < valid, lf_r, 0.0)
    eye = row == col
    ig_c = jnp.sum(jnp.where(eye, ig_r, 0.0), axis=1, keepdims=True)
    lf_c = jnp.sum(jnp.where(eye, lf_r, 0.0), axis=1, keepdims=True)
    causal = row >= col
    b_c = jnp.sum(jnp.where(causal, lf_r, 0.0), axis=1, keepdims=True)
    b_r = jnp.sum(jnp.where(row <= col, lf_c, 0.0), axis=0, keepdims=True)
    b_last = jnp.sum(lf_r, axis=1, keepdims=True)

    m_st = m_sc[...]
    d = jnp.where(causal, b_c - b_r + ig_r, NEG)
    m_inter = b_c + m_st
    m_t = jnp.maximum(m_inter, jnp.max(d, axis=1, keepdims=True))
    q = q_ref[...]
    s = jnp.dot(q, kt_ref[0], preferred_element_type=F32)
    s = s * jnp.exp(d - m_t)
    w_inter = jnp.exp(m_inter - m_t)
    c_st = c_sc[...]
    n_st = n_sc[...]
    v = v_ref[...]
    num = (jnp.dot(s.astype(BF16), v, preferred_element_type=F32)
           + w_inter * jnp.dot(q, c_st.astype(BF16), preferred_element_type=F32))
    qn = jnp.sum(q.astype(F32) * n_st, axis=1, keepdims=True)
    den = jnp.sum(s, axis=1, keepdims=True) + w_inter * qn
    hcell = num / jnp.maximum(jnp.abs(den), jnp.exp(-m_t))

    m_new = m_t[lc - 1:lc, :]
    w_r = jnp.exp(b_last - b_r + ig_r - m_new)
    w_c = jnp.exp(b_last - b_c + ig_c - m_new)
    decay = jnp.exp(b_last + m_st - m_new)
    kw_t = (kt_ref[0].astype(F32) * w_r).astype(BF16)
    c_new = decay * c_st + jnp.dot(kw_t, v, preferred_element_type=F32)
    n_new = decay * n_st + jnp.sum(k_ref[...].astype(F32) * w_c, axis=0, keepdims=True)
    c_sc[...] = c_new
    n_sc[...] = n_new
    m_sc[...] = m_new

    mu = jnp.mean(hcell, axis=1, keepdims=True)
    hc = hcell - mu
    var = jnp.mean(hc * hc, axis=1, keepdims=True)
    hn = hc * lax.rsqrt(var + EPS) * mh_ref[...]
    z = z_ref[...]
    o_ref[...] = ((hn + skip_ref[...] * xc_ref[...].astype(F32)) * (z * _sigmoid(z))).astype(BF16)

    @pl.when(c == pl.num_programs(2) - 1)
    def _():
        cout_ref[0, 0, 0] = c_new
        nout_ref[0, 0] = n_new
        mout_ref[0, 0] = jnp.broadcast_to(m_new, (1, 128))


def mlstm_cell(q, k, kt, v, xc, up, gates, lw, nseq, nchunk, lc, valid, layer, nlayer, init, c_prev):
    hd = A_HEAD_DIM
    nt = nseq * nchunk
    g = gates[:, :2 * A_HEADS].reshape(nt, lc, 2 * A_HEADS)
    g_row = jnp.transpose(g, (0, 2, 1))[:, :, None, :]
    row_spec = pl.BlockSpec((lc, hd), lambda s, h, c: (s * nchunk + c, h))
    in_specs = [row_spec, row_spec,
                pl.BlockSpec((1, hd, lc), lambda s, h, c: (s * nchunk + c, h, 0)),
                row_spec, row_spec,
                pl.BlockSpec((lc, hd), lambda s, h, c: (s * nchunk + c, A_HEADS + h)),
                pl.BlockSpec((1, 1, 1, lc), lambda s, h, c: (s * nchunk + c, h, 0, 0)),
                pl.BlockSpec((1, 1, 1, lc), lambda s, h, c: (s * nchunk + c, A_HEADS + h, 0, 0)),
                pl.BlockSpec((1, hd), lambda s, h, c: (0, h)),
                pl.BlockSpec((1, hd), lambda s, h, c: (0, h))]
    args = [q, k, kt, v, xc, up, g_row, g_row, lw["mh_w"], lw["skip"]]
    has_init = init is not None
    if has_init:
        c0, n0, m0 = init
        in_specs += [pl.BlockSpec((1, 1, 1, hd, hd), lambda s, h, c: (layer, s, h, 0, 0)),
                     pl.BlockSpec((1, 1, 1, 1, hd), lambda s, h, c: (layer, s, h, 0, 0)),
                     pl.BlockSpec((1, 1, 1, 1, 1), lambda s, h, c: (layer, s, h, 0, 0))]
        args += [c0, n0.reshape(n0.shape[:3] + (1, hd)), m0.reshape(m0.shape + (1, 1))]
    has_prev = c_prev is not None
    aliases = {}
    if has_prev:
        aliases = {len(args): 1}
        in_specs.append(pl.BlockSpec(memory_space=pl.ANY))
        args.append(c_prev)
    out, c_out, n_out, m_out = pl.pallas_call(
        functools.partial(_mlstm_cell_kernel, lc=lc, valid=valid, has_init=has_init, has_prev=has_prev),
        grid=(nseq, A_HEADS, nchunk),
        in_specs=in_specs,
        out_specs=[row_spec,
                   pl.BlockSpec((1, 1, 1, hd, hd), lambda s, h, c: (layer, s, h, 0, 0)),
                   pl.BlockSpec((1, 1, 1, hd), lambda s, h, c: (s, h, 0, 0)),
                   pl.BlockSpec((1, 1, 1, 128), lambda s, h, c: (s, h, 0, 0))],
        out_shape=[jax.ShapeDtypeStruct((nt * lc, A_INNER), BF16),
                   jax.ShapeDtypeStruct((nlayer, nseq, A_HEADS, hd, hd), F32),
                   jax.ShapeDtypeStruct((nseq, A_HEADS, 1, hd), F32),
                   jax.ShapeDtypeStruct((nseq, A_HEADS, 1, 128), F32)],
        scratch_shapes=[pltpu.VMEM((hd, hd), F32), pltpu.VMEM((1, hd), F32), pltpu.VMEM((1, 1), F32)],
        input_output_aliases=aliases,
        compiler_params=_params(("parallel", "parallel", "arbitrary"), 40),
        name="mlstm_cell",
    )(*args)
    return out, c_out, n_out[:, :, 0, :], m_out[:, :, 0, 0]


def _mlstm_weights(layer, a_norm_pre, a_w_up, a_conv_w, a_conv_b, a_w_q, a_w_k, a_w_v, a_w_if, a_b_if,
                   a_mh_w, a_skip, a_w_down, a_norm_post):
    wif = a_w_if[layer].reshape(3, A_INNER, 2 * A_HEADS)
    wif = jnp.pad(wif, ((0, 0), (0, 0), (0, 128 - 2 * A_HEADS))).astype(BF16)
    return {
        "norm_pre": a_norm_pre[layer],
        "w_up": a_w_up[layer].astype(BF16),
        "conv_w8": jnp.pad(a_conv_w[layer], ((0, 8 - A_CONV), (0, 0))),
        "conv_b": a_conv_b[layer].reshape(1, A_INNER),
        "bd": jnp.stack([_headwise_tiles(a_w_q[layer]), _headwise_tiles(a_w_k[layer]),
                         _headwise_tiles(a_w_v[layer])]).astype(BF16),
        "wif": wif,
        "bif": jnp.pad(a_b_if[layer], (0, 128 - 2 * A_HEADS)).reshape(1, 128),
        "mh_w": a_mh_w[layer].reshape(1, A_INNER),
        "skip": a_skip[layer].reshape(1, A_INNER),
        "w_down": a_w_down[layer].astype(BF16),
        "norm_post": a_norm_post[layer],
    }


def mlstm_layer_prompt(x, lw, nseq, seq, layer, nlayer, c_prev):
    lc = CHUNK_P
    nchunk = seq // lc
    up = norm_matmul(x, lw["norm_pre"], lw["w_up"], TM_PROJ, 1024)
    q, k, v, xc, gates, kt = mlstm_pre(up, up, nseq, nchunk, lc, 1, lw, zero_first=True, emit_kt=True)
    out, c, n, m = mlstm_cell(q, k, kt, v, xc, up, gates, lw, nseq, nchunk, lc, lc, layer, nlayer, None, c_prev)
    x_new = matmul_norm_res(out, lw["w_down"], lw["norm_post"], x, 512)
    conv_new = up.reshape(nseq, seq, 2 * A_INNER)[:, seq - (A_CONV - 1):, :A_INNER]
    return x_new, c, n, m, conv_new


def mlstm_layer_sample(x, lw, nseq, ds, layer, state, conv0, c_prev):
    nsub = 32
    assert nseq % nsub == 0
    up = norm_matmul(x, lw["norm_pre"], lw["w_up"], TM_PROJ, 1024)
    halo = jnp.pad(conv0, ((0, 0), (8 - (A_CONV - 1), 0), (0, 0))).reshape(nseq * 8, A_INNER)
    q, k, v, xc, gates = mlstm_pre(up, halo, nseq // nsub, 1, SPAD, nsub, lw, zero_first=False, emit_kt=False)
    kt = jnp.transpose(k.reshape(nseq, SPAD, A_INNER), (0, 2, 1))
    out, c, n, m = mlstm_cell(q, k, kt, v, xc, up, gates, lw, nseq, 1, SPAD, ds, layer, state[0].shape[0],
                              state, c_prev)
    x_new = matmul_norm_res(out, lw["w_down"], lw["norm_post"], x, 512)
    xm = up.reshape(nseq, SPAD, 2 * A_INNER)[:, :ds, :A_INNER]
    conv_new = jnp.concatenate([conv0, xm], axis=1)[:, ds:]
    return x_new, c, n, m, conv_new


CMP_PACK = 4


def _compress_kernel(pt_ref, *refs, npages, feature_major):
    del pt_ref
    pages = refs[:npages]
    pos_ref, w1_ref, b1_ref, w2_ref, b2_ref, o_ref, stage_ref, acc_ref = refs[npages:]
    nblk = acc_ref.shape[1]
    ncb = stage_ref.shape[0]
    for p in range(npages):
        for cb in range(ncb):
            if feature_major:
                blk = pages[p][0, cb * 128:(cb + 1) * 128, :].T
            else:
                blk = pages[p][0, :, cb * 128:(cb + 1) * 128]
            stage_ref[cb, p * PAGE:(p + 1) * PAGE, :] = blk

    low = _low_half(nblk)
    for quad in range(CMP_STRIDE // CMP_PACK):
        for cb in range(ncb):
            xs = [stage_ref[cb, pl.ds(CMP_PACK * quad + i, nblk, stride=CMP_STRIDE), :] for i in range(CMP_PACK)]
            for gg in range(2):
                if gg == 0:
                    tiles = [jnp.where(low, xs[i], pltpu.roll(xs[i + 1], B_HEAD_DIM, axis=1))
                             for i in range(0, CMP_PACK, 2)]
                else:
                    tiles = [jnp.where(low, pltpu.roll(xs[i], B_HEAD_DIM, axis=1), xs[i + 1])
                             for i in range(0, CMP_PACK, 2)]
                packed = jnp.concatenate(tiles, axis=1)
                c = 2 * cb + gg
                slot = c // B_KV_HEADS
                for half in range(2):
                    idx = half * (CMP_STRIDE // CMP_PACK) + quad
                    xb = (packed + pos_ref[slot, idx]).astype(BF16)
                    d = jnp.dot(xb, w1_ref[slot, idx], preferred_element_type=F32)
                    if quad == 0:
                        acc_ref[2 * c + half] = d
                    else:
                        acc_ref[2 * c + half] += d

    out = jnp.zeros((nblk, 2 * B_KV_HEADS * B_HEAD_DIM), F32)
    for c in range(2 * B_KV_HEADS):
        slot = c // B_KV_HEADS
        nxt = pltpu.roll(acc_ref[2 * c + 1], nblk - 1, axis=0)
        hpre = acc_ref[2 * c] + nxt + b1_ref[slot]
        hid = hpre * _sigmoid(hpre)
        out = out + jnp.dot(hid.astype(BF16), w2_ref[c], preferred_element_type=F32)
    out = out + b2_ref[...]
    rid = lax.broadcasted_iota(jnp.int32, (nblk, 1), 0)
    o_ref[0] = jnp.where(rid < nblk - 1, out, 0.0)


def compress(pages_arr, table, nb, cw, feature_major):
    npages = table.shape[0] // nb
    nblk = npages * (PAGE // CMP_STRIDE)
    width = 2 * B_KV_HEADS * B_HEAD_DIM
    npk = CMP_LEN // CMP_PACK

    def page_map(b, pt, p):
        return (pt[b * npages + p], 0, 0)

    def const(shape):
        return pl.BlockSpec(shape, lambda b, pt: (0,) * len(shape))

    page_block = (1, width, PAGE) if feature_major else (1, PAGE, width)
    grid_spec = pltpu.PrefetchScalarGridSpec(
        num_scalar_prefetch=1,
        grid=(nb,),
        in_specs=[pl.BlockSpec(page_block, functools.partial(page_map, p=p)) for p in range(npages)]
        + [const((2, npk, 1, CMP_PACK * B_HEAD_DIM)), const((2, npk, CMP_PACK * B_HEAD_DIM, CMP_HIDDEN)),
           const((2, 1, CMP_HIDDEN)), const((2 * B_KV_HEADS, CMP_HIDDEN, width)), const((1, width))],
        out_specs=pl.BlockSpec((1, nblk, width), lambda b, pt: (b, 0, 0)),
        scratch_shapes=[pltpu.VMEM((width // 128, npages * PAGE, 128), F32),
                        pltpu.VMEM((4 * B_KV_HEADS, nblk, CMP_HIDDEN), F32)],
    )
    return pl.pallas_call(
        functools.partial(_compress_kernel, npages=npages, feature_major=feature_major),
        grid_spec=grid_spec,
        out_shape=jax.ShapeDtypeStruct((nb, nblk, width), F32),
        compiler_params=_params(("arbitrary",), 48),
        name="nsa_compress",
    )(table, *([pages_arr] * npages), cw["pos"], cw["w1"], cw["b1"], cw["w2p"], cw["b2p"])


def _compress_weights(cmp_pos, cmp_w1, cmp_b1, cmp_w2, cmp_b2):
    width = 2 * B_KV_HEADS * B_HEAD_DIM
    npk = CMP_LEN // CMP_PACK
    w2p = jnp.zeros((2 * B_KV_HEADS, CMP_HIDDEN, width), F32)
    for c in range(2 * B_KV_HEADS):
        w2p = w2p.at[c, :, c * B_HEAD_DIM:(c + 1) * B_HEAD_DIM].set(cmp_w2[c // B_KV_HEADS])
    return {
        "pos": cmp_pos.reshape(2, npk, 1, CMP_PACK * B_HEAD_DIM),
        "w1": cmp_w1.reshape(2, npk, CMP_PACK * B_HEAD_DIM, CMP_HIDDEN).astype(BF16),
        "b1": cmp_b1.reshape(2, 1, CMP_HIDDEN),
        "w2p": w2p.astype(BF16),
        "b2p": jnp.tile(cmp_b2[:, None, :], (1, B_KV_HEADS, 1)).reshape(1, width),
    }


def _softmax_masked(s, mask):
    s = jnp.where(mask, s, NEG)
    e = jnp.exp(s - jnp.max(s, axis=1, keepdims=True))
    return jnp.where(mask, e / jnp.sum(e, axis=1, keepdims=True), 0.0)


def _dot_nt(a, b):
    return lax.dot_general(a, b, (((1,), (1,)), ((), ())), preferred_element_type=F32)


def _low_half(tq):
    return lax.broadcasted_iota(jnp.int32, (tq, 128), 1) < B_HEAD_DIM


def _pad_heads(qblk, gl, tq):
    low = _low_half(tq)
    keep = low if gl == 0 else jnp.logical_not(low)
    tiles = []
    for i in range(4):
        hh = gl * 4 + i
        tile = qblk[:, (hh // 2) * 128:(hh // 2 + 1) * 128]
        if hh % 2 != gl:
            tile = pltpu.roll(tile, B_HEAD_DIM, axis=1)
        tiles.append(jnp.where(keep, tile, 0.0))
    return jnp.concatenate(tiles, axis=0).astype(BF16)


def _select_blocks(psum, qpos1, cover, ns):
    hi = psum.astype(BF16)
    lo = (psum - hi.astype(F32)).astype(BF16)
    imp = jnp.dot(hi, cover, preferred_element_type=F32) + jnp.dot(lo, cover, preferred_element_type=F32)
    s_idx = lax.broadcasted_iota(jnp.int32, (1, 128), 1)
    cur = jnp.right_shift(qpos1, 6)
    assert SEL_LEN == 64
    valid = s_idx * SEL_LEN <= qpos1
    forced = (s_idx == 0) | (s_idx == cur) | (s_idx == cur - 1)
    score = jnp.where(valid, imp + jnp.where(forced, FORCE_BONUS, 0.0), NEG)
    score = jnp.where(s_idx < ns, score, 2 * NEG)
    rank = jnp.zeros_like(score)
    for sp in range(ns):
        colv = score[:, sp:sp + 1]
        ahead = (colv > score) | ((colv == score) & (s_idx > sp))
        rank = rank + jnp.where(ahead, 1.0, 0.0)
    return jnp.where((rank < min(SEL_TOPN, ns)) & (s_idx < ns), 1.0, 0.0)


def _select_blocks_t(psum, qpos_row, cover_t, ns):
    tq = psum.shape[0]
    pt = psum.T
    hi = pt.astype(BF16)
    lo = (pt - hi.astype(F32)).astype(BF16)
    imp = jnp.dot(cover_t, hi, preferred_element_type=F32) + jnp.dot(cover_t, lo, preferred_element_type=F32)
    nr = -(-ns // 8) * 8
    imp = imp[0:nr]
    s_idx = lax.broadcasted_iota(jnp.int32, (nr, 1), 0)
    cur = jnp.right_shift(qpos_row, 6)
    valid = s_idx * SEL_LEN <= qpos_row
    forced = (s_idx == 0) | (s_idx == cur) | (s_idx == cur - 1)
    score = jnp.where(valid, imp + jnp.where(forced, FORCE_BONUS, 0.0), NEG)
    score = jnp.where(s_idx < ns, score, 2 * NEG)
    rank = jnp.zeros_like(score)
    for sp in range(ns):
        rowv = score[sp:sp + 1, :]
        ahead = (rowv > score) | ((rowv == score) & (s_idx > sp))
        rank = rank + jnp.where(ahead, 1.0, 0.0)
    sel_t = jnp.where((rank < min(SEL_TOPN, ns)) & (s_idx < ns), 1.0, 0.0)
    return jnp.concatenate([sel_t, jnp.zeros((128 - nr, tq), F32)], axis=0).T


def _cmp_branch(qs, tq, qpos, kc, vc):
    n_idx = lax.broadcasted_iota(jnp.int32, (1, 128), 1)
    p_c = _softmax_masked(_dot_nt(qs, kc), n_idx * CMP_STRIDE + (CMP_LEN - 1) <= qpos)
    o_c = jnp.dot(p_c.astype(BF16), vc, preferred_element_type=F32)
    return o_c, p_c[0:tq] + p_c[tq:2 * tq] + p_c[2 * tq:3 * tq] + p_c[3 * tq:4 * tq]


def _softmax_pv(s, tq, bias, pv):
    nk = s.shape[1]
    s3 = s.reshape(4, tq, nk) + bias[None]
    e = jnp.exp(s3 - jnp.max(s3, axis=2, keepdims=True))
    den = jnp.sum(e, axis=2, keepdims=True)
    o = pv(e.reshape(4 * tq, nk).astype(BF16))
    return o * (1.0 / den.reshape(4 * tq, 1))


def _combine(o_heads, sg, zs, tq):
    low = _low_half(tq)
    y = jnp.zeros((tq, 512), F32)
    for br in range(3):
        tiles = []
        for j in range(4):
            c = br * 8 + 2 * j
            a = o_heads[br][2 * j] * sg[:, c:c + 1]
            b = o_heads[br][2 * j + 1] * sg[:, c + 1:c + 2]
            if j // 2 == 0:
                tiles.append(jnp.where(low, a, pltpu.roll(b, B_HEAD_DIM, axis=1)))
            else:
                tiles.append(jnp.where(low, pltpu.roll(a, B_HEAD_DIM, axis=1), b))
        z = zs[br].astype(F32)
        y = y + jnp.concatenate(tiles, axis=1) * (z * _sigmoid(z))
    return y


def _nsa_prompt_kernel(q_ref, z0_ref, z1_ref, z2_ref, gate_ref, kc_ref, vc_ref, ks_ref, vs_ref, kw_ref, vw_ref,
                       cover_t_ref, blockneg_ref, y_ref, osel_sc, kaug_sc, *, tq, seq, kstep):
    q0 = pl.program_id(2) * tq
    qblk = q_ref[...].astype(F32) * (B_HEAD_DIM ** -0.5)
    qpos1 = q0 + lax.broadcasted_iota(jnp.int32, (tq, 1), 0)
    qpos_row = q0 + lax.broadcasted_iota(jnp.int32, (1, tq), 1)
    qpos = jnp.concatenate([qpos1] * 4, axis=0)
    wlen = WINDOW + tq
    wstart = pl.multiple_of(jnp.maximum(q0 - WINDOW, 0), 128)
    kw = kw_ref[pl.ds(wstart, wlen), :]
    vw = vw_ref[pl.ds(wstart, wlen), :]
    kpos_w = wstart + lax.broadcasted_iota(jnp.int32, (1, wlen), 1)
    wbias = jnp.where((kpos_w <= qpos1) & (kpos_w > qpos1 - WINDOW), 0.0, NEG)
    kc = kc_ref[0].astype(BF16)
    vc = vc_ref[0].astype(BF16)

    @pl.when(pl.program_id(2) == 0)
    def _():
        kaug_sc[:, 0:128] = ks_ref[...]
        kaug_sc[:, 128:256] = blockneg_ref[...]

    nbefore = (q0 + kstep - 1) // kstep
    first_blk = q0 // SEL_LEN
    kd = ks_ref[pl.ds(pl.multiple_of(q0, 128), tq), :]
    vd = vs_ref[pl.ds(pl.multiple_of(q0, 128), tq), :]
    tri = jnp.where(lax.broadcasted_iota(jnp.int32, (tq, tq), 1) <= lax.broadcasted_iota(jnp.int32, (tq, tq), 0),
                    0.0, NEG)
    s_idx = lax.broadcasted_iota(jnp.int32, (1, 128), 1)
    o_heads = [[None] * 8 for _ in range(3)]
    for gl in range(2):
        qs = _pad_heads(qblk, gl, tq)
        o_c, psum = _cmp_branch(qs, tq, qpos, kc, vc)
        sel = _select_blocks_t(psum, qpos_row, cover_t_ref[...], seq // SEL_LEN)
        skip = jnp.where((sel > 0.5) & (s_idx < first_blk), 0.0, 1.0).astype(BF16)
        q_aug = jnp.concatenate([qs, jnp.concatenate([skip] * 4, axis=0)], axis=1)
        s_d = _dot_nt(qs, kd).reshape(4, tq, tq) + tri[None]
        m_d = jnp.max(s_d, axis=2, keepdims=True)

        @pl.when(nbefore == 0)
        def _(gl=gl, s_d=s_d, m_d=m_d):
            e_d = jnp.exp(s_d - m_d)
            den = jnp.sum(e_d, axis=2, keepdims=True)
            o = jnp.dot(e_d.reshape(4 * tq, tq).astype(BF16), vd, preferred_element_type=F32)
            osel_sc[gl] = o * (1.0 / den.reshape(4 * tq, 1))

        for v in range(1, seq // kstep + 1):
            @pl.when(nbefore == v)
            def _(nk=v * kstep, gl=gl, q_aug=q_aug, s_d=s_d, m_d=m_d):
                s_b = _dot_nt(q_aug, kaug_sc[0:nk, :]).reshape(4, tq, nk)
                m = jnp.maximum(m_d, jnp.max(s_b, axis=2, keepdims=True))
                e_b = jnp.exp(s_b - m)
                e_d = jnp.exp(s_d - m)
                den = jnp.sum(e_b, axis=2, keepdims=True) + jnp.sum(e_d, axis=2, keepdims=True)
                o = (jnp.dot(e_b.reshape(4 * tq, nk).astype(BF16), vs_ref[0:nk, :], preferred_element_type=F32)
                     + jnp.dot(e_d.reshape(4 * tq, tq).astype(BF16), vd, preferred_element_type=F32))
                osel_sc[gl] = o * (1.0 / den.reshape(4 * tq, 1))

        o_w = _softmax_pv(_dot_nt(qs, kw), tq, wbias, lambda e: jnp.dot(e, vw, preferred_element_type=F32))
        outs = (o_c, osel_sc[gl], o_w)
        for br in range(3):
            for i in range(4):
                o_heads[br][gl * 4 + i] = outs[br][i * tq:(i + 1) * tq]
    y = _combine(o_heads, _sigmoid(gate_ref[...]), (z0_ref[...], z1_ref[...], z2_ref[...]), tq)
    y_ref[...] = y.astype(BF16)


def _cover_matrix(transposed=False):
    n = np.arange(128)[:, None]
    s = np.arange(128)[None, :]
    cov = (n * CMP_STRIDE < s * SEL_LEN + SEL_LEN) & (n * CMP_STRIDE + CMP_LEN - 1 >= s * SEL_LEN)
    return jnp.asarray(cov.T if transposed else cov, BF16)


def _expand_matrix(nkeys):
    s = np.arange(128)[:, None]
    key = np.arange(nkeys)[None, :]
    return jnp.asarray(key // SEL_LEN == s, BF16)


def nsa_attend_prompt(pm, pg, cmp, kvb, nb, seq):
    tq = 128
    nq = seq // tq

    def rows(width, col):
        return pl.BlockSpec((tq, width), lambda b, gp, t: (b * nq + t, col(gp)))

    def seq_block(first):
        return pl.BlockSpec((seq, 128), lambda b, gp, t: (b, first + gp))

    kstep = 256
    assert seq % kstep == 0 and tq % 128 == 0
    return pl.pallas_call(
        functools.partial(_nsa_prompt_kernel, tq=tq, seq=seq, kstep=kstep),
        grid=(nb, 2, nq),
        in_specs=[rows(512, lambda gp: gp), rows(512, lambda gp: 2 + gp), rows(512, lambda gp: 4 + gp),
                  rows(512, lambda gp: 6 + gp), rows(128, lambda gp: gp),
                  pl.BlockSpec((1, 128, 128), lambda b, gp, t: (b, 0, gp)),
                  pl.BlockSpec((1, 128, 128), lambda b, gp, t: (b, 0, 2 + gp)),
                  seq_block(4), seq_block(6), seq_block(8), seq_block(10),
                  pl.BlockSpec((128, 128), lambda b, gp, t: (0, 0)),
                  pl.BlockSpec((seq, 128), lambda b, gp, t: (0, 0))],
        out_specs=rows(512, lambda gp: gp),
        out_shape=jax.ShapeDtypeStruct((nb * seq, B_ATT), BF16),
        scratch_shapes=[pltpu.VMEM((2, 4 * tq, 128), F32), pltpu.VMEM((seq, 256), BF16)],
        compiler_params=_params(("parallel", "parallel", "arbitrary"), 56),
        name="nsa_attend_prompt",
    )(pm, pm, pm, pm, pg, cmp, cmp, kvb, kvb, kvb, kvb, _cover_matrix(transposed=True),
      NEG * _expand_matrix(seq).T)


def _nsa_sample_kernel(pt_ref, *refs, npages, past):
    del pt_ref
    pages = refs[:npages]
    (q_ref, z0_ref, z1_ref, z2_ref, gate_ref, cmp_ref, new_ref, win_ref, cover_ref, expand_ref, y_ref,
     kst_sc, vst_sc) = refs[npages:]
    tq = SPAD
    ngp = 2
    rows_gp = 8 * tq
    wrows = win_ref.shape[2]
    ns = past // SEL_LEN + 1
    sg = _sigmoid(gate_ref[...])

    def qpos_rows(n):
        return past + jnp.bitwise_and(lax.broadcasted_iota(jnp.int32, (n, 1), 0), tq - 1)

    def new_rows(col):
        blk = new_ref[:, col:col + 128].astype(BF16)
        return jnp.concatenate([blk, jnp.zeros((NEW_PAD - tq, 128), BF16)], axis=0)

    def per_pair(fn):
        return jnp.concatenate([fn(gp) for gp in range(ngp)], axis=0)

    for gp in range(ngp):
        lo, hi = gp * 128, (gp + 1) * 128
        for p in range(npages):
            kst_sc[gp, :, p * PAGE:(p + 1) * PAGE] = pages[p][0, lo:hi, :].astype(BF16)
            vst_sc[gp, :, p * PAGE:(p + 1) * PAGE] = pages[p][0, 256 + lo:256 + hi, :].astype(BF16)
    qs = []
    for gp in range(ngp):
        qblk = q_ref[:, gp * 512:(gp + 1) * 512].astype(F32) * (B_HEAD_DIM ** -0.5)
        qs.append(jnp.concatenate([_pad_heads(qblk, 0, tq), _pad_heads(qblk, 1, tq)], axis=0))

    n_idx = lax.broadcasted_iota(jnp.int32, (1, 128), 1)
    s = per_pair(lambda gp: _dot_nt(qs[gp], cmp_ref[0, :, gp * 128:(gp + 1) * 128].astype(BF16)))
    p_c = _softmax_masked(s, n_idx * CMP_STRIDE + (CMP_LEN - 1) <= qpos_rows(ngp * rows_gp))
    p_cb = p_c.astype(BF16)
    o_c = [jnp.dot(p_cb[gp * rows_gp:(gp + 1) * rows_gp],
                   cmp_ref[0, :, 256 + gp * 128:256 + (gp + 1) * 128].astype(BF16), preferred_element_type=F32)
           for gp in range(ngp)]
    psum = jnp.concatenate([sum(p_c[(g * 4 + i) * tq:(g * 4 + i + 1) * tq] for i in range(4))
                            for g in range(2 * ngp)], axis=0)
    nsel_rows = 2 * ngp * tq
    sel = _select_blocks(psum, qpos_rows(nsel_rows), cover_ref[...], ns).astype(BF16)
    selk = jnp.dot(sel, expand_ref[...], preferred_element_type=F32) > 0.5
    kpos_s = lax.broadcasted_iota(jnp.int32, (1, past + NEW_PAD), 1)
    sbias = jnp.where(selk & (kpos_s <= qpos_rows(nsel_rows)), 0.0, NEG)
    sbias = jnp.concatenate([sbias[g * tq:(g + 1) * tq] for g in range(2 * ngp) for _ in range(4)], axis=0)
    s = per_pair(lambda gp: jnp.concatenate(
        [jnp.dot(qs[gp], kst_sc[gp], preferred_element_type=F32),
         _dot_nt(qs[gp], new_rows(512 + gp * 128))], axis=1)) + sbias
    e = jnp.exp(s - jnp.max(s, axis=1, keepdims=True))
    rden = 1.0 / jnp.sum(e, axis=1, keepdims=True)
    e = e.astype(BF16)
    o_s = [(_dot_nt(e[gp * rows_gp:(gp + 1) * rows_gp, 0:past], vst_sc[gp])
            + jnp.dot(e[gp * rows_gp:(gp + 1) * rows_gp, past:], new_rows(768 + gp * 128),
                      preferred_element_type=F32)) * rden[gp * rows_gp:(gp + 1) * rows_gp]
           for gp in range(ngp)]
    kpos_w = (past - wrows) + lax.broadcasted_iota(jnp.int32, (1, wrows + NEW_PAD), 1)
    qpos = qpos_rows(ngp * rows_gp)
    wbias = jnp.where((kpos_w <= qpos) & (kpos_w > qpos - WINDOW) & (kpos_w >= 0), 0.0, NEG)
    s = per_pair(lambda gp: jnp.concatenate(
        [jnp.dot(qs[gp], win_ref[0, gp * 128:(gp + 1) * 128, :].astype(BF16), preferred_element_type=F32),
         _dot_nt(qs[gp], new_rows(1024 + gp * 128))], axis=1)) + wbias
    e = jnp.exp(s - jnp.max(s, axis=1, keepdims=True))
    rden = 1.0 / jnp.sum(e, axis=1, keepdims=True)
    e = e.astype(BF16)
    o_w = [(_dot_nt(e[gp * rows_gp:(gp + 1) * rows_gp, 0:wrows],
                    win_ref[0, 256 + gp * 128:256 + (gp + 1) * 128, :].astype(BF16))
            + jnp.dot(e[gp * rows_gp:(gp + 1) * rows_gp, wrows:], new_rows(1280 + gp * 128),
                      preferred_element_type=F32)) * rden[gp * rows_gp:(gp + 1) * rows_gp]
           for gp in range(ngp)]

    for gp in range(ngp):
        o_heads = [[outs[gp][hh * tq:(hh + 1) * tq] for hh in range(8)] for outs in (o_c, o_s, o_w)]
        zs = (z0_ref[:, gp * 512:(gp + 1) * 512], z1_ref[:, gp * 512:(gp + 1) * 512],
              z2_ref[:, gp * 512:(gp + 1) * 512])
        y_ref[:, gp * 512:(gp + 1) * 512] = _combine(o_heads, sg[:, gp * 128:(gp + 1) * 128], zs, tq).astype(BF16)


def nsa_attend_sample(pm, pg, cmp, kv_new, cache_pages, table, cache_win, nb, past):
    npages = past // PAGE
    wrows = cache_win.shape[2]
    nks = past + NEW_PAD
    assert past % SEL_LEN == 0 and SPAD <= SEL_LEN

    def page_map(b, pt, p):
        return (pt[b * npages + p], 1, 0)

    def rows(width, col):
        return pl.BlockSpec((SPAD, width), lambda b, pt: (b, col))

    grid_spec = pltpu.PrefetchScalarGridSpec(
        num_scalar_prefetch=1,
        grid=(nb,),
        in_specs=[pl.BlockSpec((1, 512, PAGE), functools.partial(page_map, p=p)) for p in range(npages)]
        + [rows(B_ATT, 0), rows(B_ATT, 1), rows(B_ATT, 2), rows(B_ATT, 3), rows(256, 0),
           pl.BlockSpec((1, 128, 512), lambda b, pt: (b, 0, 0)),
           rows(1536, 0),
           pl.BlockSpec((1, 512, wrows), lambda b, pt: (b, 0, 0)),
           pl.BlockSpec((128, 128), lambda b, pt: (0, 0)),
           pl.BlockSpec((128, nks), lambda b, pt: (0, 0))],
        out_specs=rows(B_ATT, 0),
        scratch_shapes=[pltpu.VMEM((2, 128, past), BF16), pltpu.VMEM((2, 128, past), BF16)],
    )
    return pl.pallas_call(
        functools.partial(_nsa_sample_kernel, npages=npages, past=past),
        grid_spec=grid_spec,
        out_shape=jax.ShapeDtypeStruct((nb * SPAD, B_ATT), BF16),
        compiler_params=_params(("arbitrary",), 48),
        name="nsa_attend_sample",
    )(table, *([cache_pages] * npages), pm, pm, pm, pm, pg, cmp, kv_new, cache_win,
      _cover_matrix(), _expand_matrix(nks))


def _nsa_weights(lb, b_norm_pre, b_w_in, b_w_out, b_norm_post):
    w_in = b_w_in[lb]
    wg = w_in[:, 4 * B_ATT:].reshape(D_MODEL, 3, 2, 8)
    wg = jnp.transpose(wg, (0, 2, 1, 3)).reshape(D_MODEL, 2, 24)
    wg = jnp.pad(wg, ((0, 0), (0, 0), (0, 128 - 24))).reshape(D_MODEL, 256)
    return {"norm_pre": b_norm_pre[lb], "w_main": w_in[:, :4 * B_ATT].astype(BF16), "w_gate": wg.astype(BF16),
            "w_out": b_w_out[lb].astype(BF16), "norm_post": b_norm_post[lb]}


def kernel(x_prompt, x_sample, cache_kv, page_table, cache_win, state_C, state_n, state_m, state_conv,
           a_norm_pre, a_w_up, a_conv_w, a_conv_b, a_w_q, a_w_k, a_w_v, a_w_if, a_b_if,
           a_mh_w, a_skip, a_w_down, a_norm_post,
           kv_norm, w_kv, cmp_pos, cmp_w1, cmp_b1, cmp_w2, cmp_b2,
           b_norm_pre, b_w_in, b_w_out, b_norm_post):
    bp, sp = x_prompt.shape[0], x_prompt.shape[1]
    bs, ds = x_sample.shape[0], x_sample.shape[1]
    n_a = state_C.shape[0]
    n_b = b_w_in.shape[0]
    past = page_table.shape[1] * cache_kv.shape[1]
    assert cache_kv.shape[1] == PAGE and sp % PAGE == 0 and ds <= SPAD
    assert (past + ds - CMP_LEN) // CMP_STRIDE + 1 == past // CMP_STRIDE - 1
    xp = x_prompt.reshape(bp * sp, D_MODEL)
    xs = jnp.pad(x_sample, ((0, 0), (0, SPAD - ds), (0, 0))).reshape(bs * SPAD, D_MODEL)

    np_l, ns_l, mp_l, ms_l, vp_l, vs_l = [], [], [], [], [], []
    c_prompt = c_sample = None
    for layer in range(n_a):
        lw = _mlstm_weights(layer, a_norm_pre, a_w_up, a_conv_w, a_conv_b, a_w_q, a_w_k, a_w_v, a_w_if, a_b_if,
                            a_mh_w, a_skip, a_w_down, a_norm_post)
        xp, c_prompt, n1, m1, v1 = mlstm_layer_prompt(xp, lw, bp, sp, layer, n_a, c_prompt)
        xs, c_sample, n2, m2, v2 = mlstm_layer_sample(xs, lw, bs, ds, layer, (state_C, state_n, state_m),
                                                      state_conv[layer], c_sample)
        np_l.append(n1); mp_l.append(m1); vp_l.append(v1)
        ns_l.append(n2); ms_l.append(m2); vs_l.append(v2)

    nkv = w_kv.shape[1]
    w_kv_b = w_kv.astype(BF16)
    kvp, kvp_b = norm_matmul(xp, kv_norm, w_kv_b, TM_PROJ, 512, also_bf16=True)
    kvs = norm_matmul(xs, kv_norm, w_kv_b, TM_PROJ, 512)
    cw = _compress_weights(cmp_pos, cmp_w1, cmp_b1, cmp_w2, cmp_b2)
    table_p = jnp.arange(bp * (sp // PAGE), dtype=jnp.int32)
    table_s = page_table.reshape(-1).astype(jnp.int32)
    cache_pages = jnp.transpose(cache_kv, (0, 2, 3, 4, 1)).reshape(cache_kv.shape[0], -1, PAGE)
    cmp_p = compress(kvp.reshape(bp * sp // PAGE, PAGE, nkv), table_p, bp, cw, feature_major=False)
    cmp_s = compress(cache_pages, table_s, bs, cw, feature_major=True)
    wb = cache_win.shape[1]
    win2 = jnp.transpose(cache_win, (0, 2, 3, 4, 1)).reshape(bs, -1, wb)

    for lb in range(n_b):
        bw = _nsa_weights(lb, b_norm_pre, b_w_in, b_w_out, b_norm_post)
        pm = norm_matmul(xp, bw["norm_pre"], bw["w_main"], TM_PROJ, 1024, out_dtype=BF16)
        pg = norm_matmul(xp, bw["norm_pre"], bw["w_gate"], TM_PROJ, 256)
        yp = nsa_attend_prompt(pm, pg, cmp_p, kvp_b, bp, sp)
        xp = matmul_norm_res(yp, bw["w_out"], bw["norm_post"], xp, 512)
        pm = norm_matmul(xs, bw["norm_pre"], bw["w_main"], TM_PROJ, 1024, out_dtype=BF16)
        pg = norm_matmul(xs, bw["norm_pre"], bw["w_gate"], TM_PROJ, 256)
        ys = nsa_attend_sample(pm, pg, cmp_s, kvs, cache_pages, table_s, win2, bs, past)
        xs = matmul_norm_res(ys, bw["w_out"], bw["norm_post"], xs, 512)

    y_prompt = xp.reshape(bp, sp, D_MODEL)
    y_sample = xs.reshape(bs, SPAD, D_MODEL)[:, :ds]
    kvp3 = kvp.reshape(bp, sp, nkv)
    kvs3 = kvs.reshape(bs, SPAD, nkv)[:, :ds]
    ncache = 4 * B_KV_HEADS * B_HEAD_DIM
    kv_rows_prompt = kvp3[:, :, :ncache].reshape(bp, sp, 4, B_KV_HEADS, B_HEAD_DIM)
    kv_rows_sample = kvs3[:, :, :ncache].reshape(bs, ds, 4, B_KV_HEADS, B_HEAD_DIM)
    win_prompt = kvp3[:, sp - min(WINDOW, sp):, ncache:].reshape(bp, min(WINDOW, sp), 2, B_KV_HEADS, B_HEAD_DIM)
    win_new = kvs3[:, :, ncache:].reshape(bs, ds, 2, B_KV_HEADS, B_HEAD_DIM).astype(cache_win.dtype)
    win_sample = jnp.concatenate([cache_win, win_new], axis=1)[:, ds:]
    return (y_prompt, y_sample, kv_rows_prompt, kv_rows_sample, win_prompt, win_sample,
            c_prompt, c_sample, jnp.stack(np_l), jnp.stack(ns_l),
            jnp.stack(mp_l), jnp.stack(ms_l), jnp.stack(vp_l), jnp.stack(vs_l))
```

```python
import functools

import jax
import jax.numpy as jnp
import numpy as np
from jax import lax
from jax.experimental import pallas as pl
from jax.experimental.pallas import tpu as pltpu

F32 = jnp.float32
BF16 = jnp.bfloat16

D_MODEL = 1024
A_INNER = 2048
A_HEADS = 4
A_HEAD_DIM = 512
A_CONV = 4
B_HEADS = 16
B_HEAD_DIM = 64
B_KV_HEADS = 4
B_ATT = 1024
CMP_LEN = 32
CMP_STRIDE = 16
CMP_HIDDEN = 256
SEL_LEN = 64
SEL_TOPN = 16
WINDOW = 512
PAGE = 128
FORCE_BONUS = 1.0e3
NEG = -1.0e30
EPS = 1e-6

SPAD = 16
NEW_PAD = 128
CHUNK_P = 512
TM_PROJ = 1024
V7X_VMEM_BYTES = 64 * 2**20


def _params(sem, vmem_mb):
    assert vmem_mb * 2**20 < V7X_VMEM_BYTES
    return pltpu.CompilerParams(dimension_semantics=sem, vmem_limit_bytes=vmem_mb * 2**20)


def _sigmoid(x):
    return 1.0 / (1.0 + jnp.exp(-x))


def _norm_matmul_kernel(x_ref, nw_ref, w_ref, *rest):
    o_refs, xn_ref = rest[:-1], rest[-1]

    @pl.when(pl.program_id(1) == 0)
    def _():
        x = x_ref[...]
        ms = jnp.mean(x * x, axis=-1, keepdims=True)
        xn_ref[...] = (x * lax.rsqrt(ms + EPS) * nw_ref[...]).astype(BF16)

    y = jnp.dot(xn_ref[...], w_ref[...], preferred_element_type=F32)
    for o_ref in o_refs:
        o_ref[...] = y.astype(o_ref.dtype)


def norm_matmul(x, nw, w, tm, tn, also_bf16=False, out_dtype=F32):
    m, k = x.shape
    n = w.shape[1]
    assert m % tm == 0 and n % tn == 0
    o_spec = pl.BlockSpec((tm, tn), lambda i, j: (i, j))
    out_specs = [o_spec]
    out_shape = [jax.ShapeDtypeStruct((m, n), out_dtype)]
    if also_bf16:
        out_specs.append(o_spec)
        out_shape.append(jax.ShapeDtypeStruct((m, n), BF16))
    res = pl.pallas_call(
        _norm_matmul_kernel,
        grid=(m // tm, n // tn),
        in_specs=[pl.BlockSpec((tm, k), lambda i, j: (i, 0)),
                  pl.BlockSpec((1, k), lambda i, j: (0, 0)),
                  pl.BlockSpec((k, tn), lambda i, j: (0, j))],
        out_specs=out_specs,
        out_shape=out_shape,
        scratch_shapes=[pltpu.VMEM((tm, k), BF16)],
        compiler_params=_params(("parallel", "arbitrary"), 40),
        name="norm_matmul",
    )(x, nw.reshape(1, k), w)
    return res if also_bf16 else res[0]


def _matmul_norm_res_kernel(a_ref, w_ref, nw_ref, x_ref, o_ref):
    y = jnp.dot(a_ref[...], w_ref[...], preferred_element_type=F32)
    ms = jnp.mean(y * y, axis=-1, keepdims=True)
    o_ref[...] = x_ref[...] + y * lax.rsqrt(ms + EPS) * nw_ref[...]


def matmul_norm_res(a, w, nw, x, tm):
    m, k = a.shape
    n = w.shape[1]
    assert m % tm == 0
    return pl.pallas_call(
        _matmul_norm_res_kernel,
        grid=(m // tm,),
        in_specs=[pl.BlockSpec((tm, k), lambda i: (i, 0)),
                  pl.BlockSpec((k, n), lambda i: (0, 0)),
                  pl.BlockSpec((1, n), lambda i: (0, 0)),
                  pl.BlockSpec((tm, n), lambda i: (i, 0))],
        out_specs=pl.BlockSpec((tm, n), lambda i: (i, 0)),
        out_shape=jax.ShapeDtypeStruct((m, n), F32),
        compiler_params=_params(("parallel",), 40),
        name="matmul_norm_res",
    )(a, w, nw.reshape(1, n), x)


def _mlstm_pre_kernel(xm_ref, prev_ref, cw_ref, cb_ref, bd_ref, wif_ref, bif_ref,
                      q_ref, k_ref, v_ref, xc_ref, g_ref, *rest, lsub, nsub, zero_first, emit_kt):
    if emit_kt:
        kt_ref, cat_ref = rest
    else:
        (cat_ref,) = rest
    t = pl.program_id(1)
    h = pl.program_id(2)
    x = xm_ref[...]
    prev = prev_ref[...]
    if zero_first:
        prev = jnp.where(t == 0, 0.0, prev)
    cw = cw_ref[...]
    pieces = []
    for s in range(nsub):
        base = s * (lsub + 8)
        xs = x[s * lsub:(s + 1) * lsub]
        cat_ref[base:base + 8, :] = prev[s * 8:(s + 1) * 8]
        cat_ref[base + 8:base + 8 + lsub, :] = xs
        acc = cb_ref[...] + xs * cw[A_CONV - 1:A_CONV]
        for j in range(1, A_CONV):
            acc = acc + cat_ref[base + 8 - j:base + 8 - j + lsub, :] * cw[A_CONV - 1 - j:A_CONV - j]
        pieces.append(acc)
    xconv = pieces[0] if nsub == 1 else jnp.concatenate(pieces, axis=0)
    xc = xconv * _sigmoid(xconv)

    xcb = xc.astype(BF16)
    xb = x.astype(BF16)
    ntile = x.shape[1] // 128

    def headwise(src, which):
        return jnp.concatenate(
            [jnp.dot(src[:, j * 128:(j + 1) * 128], bd_ref[which, j], preferred_element_type=F32)
             for j in range(ntile)], axis=1)

    q = headwise(xcb, 0)
    k = headwise(xcb, 1)
    v = headwise(xb, 2)

    g = (jnp.dot(q.astype(BF16), wif_ref[0], preferred_element_type=F32)
         + jnp.dot(k.astype(BF16), wif_ref[1], preferred_element_type=F32)
         + jnp.dot(v.astype(BF16), wif_ref[2], preferred_element_type=F32))

    @pl.when(h == 0)
    def _():
        g_ref[...] = bif_ref[...] + g

    @pl.when(h != 0)
    def _():
        g_ref[...] += g

    ks = k * (A_HEAD_DIM ** -0.5)
    q_ref[...] = q.astype(BF16)
    k_ref[...] = ks.astype(BF16)
    v_ref[...] = v.astype(BF16)
    xc_ref[...] = xc.astype(BF16)
    if emit_kt:
        kt_ref[0] = ks.T.astype(BF16)


def _headwise_tiles(w):
    n, c, _ = w.shape
    per = 128 // c
    wt = w.reshape(n // per, per, c, c)
    eye = jnp.eye(per, dtype=w.dtype)
    return jnp.einsum("nm,tncd->tncmd", eye, wt).reshape(n // per, 128, 128)


def mlstm_pre(up, halo, nseq, ntile, lsub, nsub, lw, zero_first, emit_kt):
    lt = nsub * lsub
    rows = nseq * ntile * lt
    hd = A_HEAD_DIM
    if zero_first:
        assert nsub == 1

        def prev_map(s, t, h):
            return (jnp.maximum((s * ntile + t) * (lt // 8) - 1, 0), h)
    else:
        assert ntile == 1

        def prev_map(s, t, h):
            return (s, h)
    wspec = pl.BlockSpec((8, hd), lambda s, t, h: (0, h))
    row_spec = pl.BlockSpec((lt, hd), lambda s, t, h: (s * ntile + t, h))
    out_specs = [row_spec, row_spec, row_spec, row_spec,
                 pl.BlockSpec((lt, 128), lambda s, t, h: (s * ntile + t, 0))]
    out_shape = [jax.ShapeDtypeStruct((rows, A_INNER), BF16)] * 4 + [jax.ShapeDtypeStruct((rows, 128), F32)]
    if emit_kt:
        out_specs.append(pl.BlockSpec((1, hd, lt), lambda s, t, h: (s * ntile + t, h, 0)))
        out_shape.append(jax.ShapeDtypeStruct((nseq * ntile, A_INNER, lt), BF16))
    return pl.pallas_call(
        functools.partial(_mlstm_pre_kernel, lsub=lsub, nsub=nsub, zero_first=zero_first, emit_kt=emit_kt),
        grid=(nseq, ntile, A_HEADS),
        in_specs=[row_spec,
                  pl.BlockSpec((8 * nsub, hd), prev_map),
                  wspec,
                  pl.BlockSpec((1, hd), lambda s, t, h: (0, h)),
                  pl.BlockSpec((3, hd // 128, 128, 128), lambda s, t, h: (0, h, 0, 0)),
                  pl.BlockSpec((3, hd, 128), lambda s, t, h: (0, h, 0)),
                  pl.BlockSpec((1, 128), lambda s, t, h: (0, 0))],
        out_specs=out_specs,
        out_shape=out_shape,
        scratch_shapes=[pltpu.VMEM((nsub * (lsub + 8), hd), F32)],
        compiler_params=_params(("parallel", "parallel", "arbitrary"), 40),
        name="mlstm_pre",
    )(up, halo, lw["conv_w8"], lw["conv_b"], lw["bd"], lw["wif"], lw["bif"])


def _mlstm_cell_kernel(*refs, lc, valid, has_init, has_prev):
    (q_ref, k_ref, kt_ref, v_ref, xc_ref, z_ref, igr_ref, fgr_ref, mh_ref, skip_ref) = refs[:10]
    rest = refs[10:]
    if has_init:
        c0_ref, n0_ref, m0_ref = rest[:3]
        rest = rest[3:]
    if has_prev:
        rest = rest[1:]
    o_ref, cout_ref, nout_ref, mout_ref, c_sc, n_sc, m_sc = rest
    c = pl.program_id(2)

    @pl.when(c == 0)
    def _():
        if has_init:
            c_sc[...] = c0_ref[0, 0, 0]
            n_sc[...] = n0_ref[0, 0, 0]
            m_sc[...] = m0_ref[0, 0, 0]
        else:
            c_sc[...] = jnp.zeros_like(c_sc)
            n_sc[...] = jnp.zeros_like(n_sc)
            m_sc[...] = jnp.full_like(m_sc, NEG)

    def logsig(x):
        return jnp.minimum(x, 0.0) - jnp.log(1.0 + jnp.exp(-jnp.abs(x)))

    ig_r = igr_ref[0, 0]
    lf_r = logsig(fgr_ref[0, 0])
    row = lax.broadcasted_iota(jnp.int32, (lc, lc), 0)
    col = lax.broadcasted_iota(jnp.int32, (lc, lc), 1)
    if valid < lc:
        cid = lax.broadcasted_iota(jnp.int32, (1, lc), 1)
        ig_r = jnp.where(cid < valid, ig_r, NEG)
        lf_r = jnp.where(cid < valid, lf_r, 0.0)
    eye = row == col
    ig_c = jnp.sum(jnp.where(eye, ig_r, 0.0), axis=1, keepdims=True)
    lf_c = jnp.sum(jnp.where(eye, lf_r, 0.0), axis=1, keepdims=True)
    causal = row >= col
    b_c = jnp.sum(jnp.where(causal, lf_r, 0.0), axis=1, keepdims=True)
    b_r = jnp.sum(jnp.where(row <= col, lf_c, 0.0), axis=0, keepdims=True)
    b_last = jnp.sum(lf_r, axis=1, keepdims=True)

    m_st = m_sc[...]
    d = jnp.where(causal, b_c - b_r + ig_r, NEG)
    m_inter = b_c + m_st
    m_t = jnp.maximum(m_inter, jnp.max(d, axis=1, keepdims=True))
    q = q_ref[...]
    s = jnp.dot(q, kt_ref[0], preferred_element_type=F32)
    s = s * jnp.exp(d - m_t)
    w_inter = jnp.exp(m_inter - m_t)
    c_st = c_sc[...]
    n_st = n_sc[...]
    v = v_ref[...]
    num = (jnp.dot(s.astype(BF16), v, preferred_element_type=F32)
           + w_inter * jnp.dot(q, c_st.astype(BF16), preferred_element_type=F32))
    qn = jnp.sum(q.astype(F32) * n_st, axis=1, keepdims=True)
    den = jnp.sum(s, axis=1, keepdims=True) + w_inter * qn
    hcell = num / jnp.maximum(jnp.abs(den), jnp.exp(-m_t))

    m_new = m_t[lc - 1:lc, :]
    w_r = jnp.exp(b_last - b_r + ig_r - m_new)
    w_c = jnp.exp(b_last - b_c + ig_c - m_new)
    decay = jnp.exp(b_last + m_st - m_new)
    kw_t = (kt_ref[0].astype(F32) * w_r).astype(BF16)
    c_new = decay * c_st + jnp.dot(kw_t, v, preferred_element_type=F32)
    n_new = decay * n_st + jnp.sum(k_ref[...].astype(F32) * w_c, axis=0, keepdims=True)
    c_sc[...] = c_new
    n_sc[...] = n_new
    m_sc[...] = m_new

    mu = jnp.mean(hcell, axis=1, keepdims=True)
    hc = hcell - mu
    var = jnp.mean(hc * hc, axis=1, keepdims=True)
    hn = hc * lax.rsqrt(var + EPS) * mh_ref[...]
    z = z_ref[...]
    o_ref[...] = ((hn + skip_ref[...] * xc_ref[...].astype(F32)) * (z * _sigmoid(z))).astype(BF16)

    @pl.when(c == pl.num_programs(2) - 1)
    def _():
        cout_ref[0, 0, 0] = c_new
        nout_ref[0, 0] = n_new
        mout_ref[0, 0] = jnp.broadcast_to(m_new, (1, 128))


def mlstm_cell(q, k, kt, v, xc, up, gates, lw, nseq, nchunk, lc, valid, layer, nlayer, init, c_prev):
    hd = A_HEAD_DIM
    nt = nseq * nchunk
    g = gates[:, :2 * A_HEADS].reshape(nt, lc, 2 * A_HEADS)
    g_row = jnp.transpose(g, (0, 2, 1))[:, :, None, :]
    row_spec = pl.BlockSpec((lc, hd), lambda s, h, c: (s * nchunk + c, h))
    in_specs = [row_spec, row_spec,
                pl.BlockSpec((1, hd, lc), lambda s, h, c: (s * nchunk + c, h, 0)),
                row_spec, row_spec,
                pl.BlockSpec((lc, hd), lambda s, h, c: (s * nchunk + c, A_HEADS + h)),
                pl.BlockSpec((1, 1, 1, lc), lambda s, h, c: (s * nchunk + c, h, 0, 0)),
                pl.BlockSpec((1, 1, 1, lc), lambda s, h, c: (s * nchunk + c, A_HEADS + h, 0, 0)),
                pl.BlockSpec((1, hd), lambda s, h, c: (0, h)),
                pl.BlockSpec((1, hd), lambda s, h, c: (0, h))]
    args = [q, k, kt, v, xc, up, g_row, g_row, lw["mh_w"], lw["skip"]]
    has_init = init is not None
    if has_init:
        c0, n0, m0 = init
        in_specs += [pl.BlockSpec((1, 1, 1, hd, hd), lambda s, h, c: (layer, s, h, 0, 0)),
                     pl.BlockSpec((1, 1, 1, 1, hd), lambda s, h, c: (layer, s, h, 0, 0)),
                     pl.BlockSpec((1, 1, 1, 1, 1), lambda s, h, c: (layer, s, h, 0, 0))]
        args += [c0, n0.reshape(n0.shape[:3] + (1, hd)), m0.reshape(m0.shape + (1, 1))]
    has_prev = c_prev is not None
    aliases = {}
    if has_prev:
        aliases = {len(args): 1}
        in_specs.append(pl.BlockSpec(memory_space=pl.ANY))
        args.append(c_prev)
    out, c_out, n_out, m_out = pl.pallas_call(
        functools.partial(_mlstm_cell_kernel, lc=lc, valid=valid, has_init=has_init, has_prev=has_prev),
        grid=(nseq, A_HEADS, nchunk),
        in_specs=in_specs,
        out_specs=[row_spec,
                   pl.BlockSpec((1, 1, 1, hd, hd), lambda s, h, c: (layer, s, h, 0, 0)),
                   pl.BlockSpec((1, 1, 1, hd), lambda s, h, c: (s, h, 0, 0)),
                   pl.BlockSpec((1, 1, 1, 128), lambda s, h, c: (s, h, 0, 0))],
        out_shape=[jax.ShapeDtypeStruct((nt * lc, A_INNER), BF16),
                   jax.ShapeDtypeStruct((nlayer, nseq, A_HEADS, hd, hd), F32),
                   jax.ShapeDtypeStruct((nseq, A_HEADS, 1, hd), F32),
                   jax.ShapeDtypeStruct((nseq, A_HEADS, 1, 128), F32)],
        scratch_shapes=[pltpu.VMEM((hd, hd), F32), pltpu.VMEM((1, hd), F32), pltpu.VMEM((1, 1), F32)],
        input_output_aliases=aliases,
        compiler_params=_params(("parallel", "parallel", "arbitrary"), 40),
        name="mlstm_cell",
    )(*args)
    return out, c_out, n_out[:, :, 0, :], m_out[:, :, 0, 0]


def _mlstm_weights(layer, a_norm_pre, a_w_up, a_conv_w, a_conv_b, a_w_q, a_w_k, a_w_v, a_w_if, a_b_if,
                   a_mh_w, a_skip, a_w_down, a_norm_post):
    wif = a_w_if[layer].reshape(3, A_INNER, 2 * A_HEADS)
    wif = jnp.pad(wif, ((0, 0), (0, 0), (0, 128 - 2 * A_HEADS))).astype(BF16)
    return {
        "norm_pre": a_norm_pre[layer],
        "w_up": a_w_up[layer].astype(BF16),
        "conv_w8": jnp.pad(a_conv_w[layer], ((0, 8 - A_CONV), (0, 0))),
        "conv_b": a_conv_b[layer].reshape(1, A_INNER),
        "bd": jnp.stack([_headwise_tiles(a_w_q[layer]), _headwise_tiles(a_w_k[layer]),
                         _headwise_tiles(a_w_v[layer])]).astype(BF16),
        "wif": wif,
        "bif": jnp.pad(a_b_if[layer], (0, 128 - 2 * A_HEADS)).reshape(1, 128),
        "mh_w": a_mh_w[layer].reshape(1, A_INNER),
        "skip": a_skip[layer].reshape(1, A_INNER),
        "w_down": a_w_down[layer].astype(BF16),
        "norm_post": a_norm_post[layer],
    }


def mlstm_layer_prompt(x, lw, nseq, seq, layer, nlayer, c_prev):
    lc = CHUNK_P
    nchunk = seq // lc
    up = norm_matmul(x, lw["norm_pre"], lw["w_up"], TM_PROJ, 1024)
    q, k, v, xc, gates, kt = mlstm_pre(up, up, nseq, nchunk, lc, 1, lw, zero_first=True, emit_kt=True)
    out, c, n, m = mlstm_cell(q, k, kt, v, xc, up, gates, lw, nseq, nchunk, lc, lc, layer, nlayer, None, c_prev)
    x_new = matmul_norm_res(out, lw["w_down"], lw["norm_post"], x, 512)
    conv_new = up.reshape(nseq, seq, 2 * A_INNER)[:, seq - (A_CONV - 1):, :A_INNER]
    return x_new, c, n, m, conv_new


def mlstm_layer_sample(x, lw, nseq, ds, layer, state, conv0, c_prev):
    nsub = 32
    assert nseq % nsub == 0
    up = norm_matmul(x, lw["norm_pre"], lw["w_up"], TM_PROJ, 1024)
    halo = jnp.pad(conv0, ((0, 0), (8 - (A_CONV - 1), 0), (0, 0))).reshape(nseq * 8, A_INNER)
    q, k, v, xc, gates = mlstm_pre(up, halo, nseq // nsub, 1, SPAD, nsub, lw, zero_first=False, emit_kt=False)
    kt = jnp.transpose(k.reshape(nseq, SPAD, A_INNER), (0, 2, 1))
    out, c, n, m = mlstm_cell(q, k, kt, v, xc, up, gates, lw, nseq, 1, SPAD, ds, layer, state[0].shape[0],
                              state, c_prev)
    x_new = matmul_norm_res(out, lw["w_down"], lw["norm_post"], x, 512)
    xm = up.reshape(nseq, SPAD, 2 * A_INNER)[:, :ds, :A_INNER]
    conv_new = jnp.concatenate([conv0, xm], axis=1)[:, ds:]
    return x_new, c, n, m, conv_new


CMP_PACK = 4


def _compress_kernel(pt_ref, *refs, npages, feature_major):
    del pt_ref
    pages = refs[:npages]
    pos_ref, w1_ref, b1_ref, w2_ref, b2_ref, o_ref, stage_ref, acc_ref = refs[npages:]
    nblk = acc_ref.shape[1]
    ncb = stage_ref.shape[0]
    for p in range(npages):
        for cb in range(ncb):
            if feature_major:
                blk = pages[p][0, cb * 128:(cb + 1) * 128, :].T
            else:
                blk = pages[p][0, :, cb * 128:(cb + 1) * 128]
            stage_ref[cb, p * PAGE:(p + 1) * PAGE, :] = blk

    low = _low_half(nblk)
    for quad in range(CMP_STRIDE // CMP_PACK):
        for cb in range(ncb):
            xs = [stage_ref[cb, pl.ds(CMP_PACK * quad + i, nblk, stride=CMP_STRIDE), :] for i in range(CMP_PACK)]
            for gg in range(2):
                if gg == 0:
                    tiles = [jnp.where(low, xs[i], pltpu.roll(xs[i + 1], B_HEAD_DIM, axis=1))
                             for i in range(0, CMP_PACK, 2)]
                else:
                    tiles = [jnp.where(low, pltpu.roll(xs[i], B_HEAD_DIM, axis=1), xs[i + 1])
                             for i in range(0, CMP_PACK, 2)]
                packed = jnp.concatenate(tiles, axis=1)
                c = 2 * cb + gg
                slot = c // B_KV_HEADS
                for half in range(2):
                    idx = half * (CMP_STRIDE // CMP_PACK) + quad
                    xb = (packed + pos_ref[slot, idx]).astype(BF16)
                    d = jnp.dot(xb, w1_ref[slot, idx], preferred_element_type=F32)
                    if quad == 0:
                        acc_ref[2 * c + half] = d
                    else:
                        acc_ref[2 * c + half] += d

    out = jnp.zeros((nblk, 2 * B_KV_HEADS * B_HEAD_DIM), F32)
    for c in range(2 * B_KV_HEADS):
        slot = c // B_KV_HEADS
        nxt = pltpu.roll(acc_ref[2 * c + 1], nblk - 1, axis=0)
        hpre = acc_ref[2 * c] + nxt + b1_ref[slot]
        hid = hpre * _sigmoid(hpre)
        out = out + jnp.dot(hid.astype(BF16), w2_ref[c], preferred_element_type=F32)
    out = out + b2_ref[...]
    rid = lax.broadcasted_iota(jnp.int32, (nblk, 1), 0)
    o_ref[0] = jnp.where(rid < nblk - 1, out, 0.0)


def compress(pages_arr, table, nb, cw, feature_major):
    npages = table.shape[0] // nb
    nblk = npages * (PAGE // CMP_STRIDE)
    width = 2 * B_KV_HEADS * B_HEAD_DIM
    npk = CMP_LEN // CMP_PACK

    def page_map(b, pt, p):
        return (pt[b * npages + p], 0, 0)

    def const(shape):
        return pl.BlockSpec(shape, lambda b, pt: (0,) * len(shape))

    page_block = (1, width, PAGE) if feature_major else (1, PAGE, width)
    grid_spec = pltpu.PrefetchScalarGridSpec(
        num_scalar_prefetch=1,
        grid=(nb,),
        in_specs=[pl.BlockSpec(page_block, functools.partial(page_map, p=p)) for p in range(npages)]
        + [const((2, npk, 1, CMP_PACK * B_HEAD_DIM)), const((2, npk, CMP_PACK * B_HEAD_DIM, CMP_HIDDEN)),
           const((2, 1, CMP_HIDDEN)), const((2 * B_KV_HEADS, CMP_HIDDEN, width)), const((1, width))],
        out_specs=pl.BlockSpec((1, nblk, width), lambda b, pt: (b, 0, 0)),
        scratch_shapes=[pltpu.VMEM((width // 128, npages * PAGE, 128), F32),
                        pltpu.VMEM((4 * B_KV_HEADS, nblk, CMP_HIDDEN), F32)],
    )
    return pl.pallas_call(
        functools.partial(_compress_kernel, npages=npages, feature_major=feature_major),
        grid_spec=grid_spec,
        out_shape=jax.ShapeDtypeStruct((nb, nblk, width), F32),
        compiler_params=_params(("arbitrary",), 48),
        name="nsa_compress",
    )(table, *([pages_arr] * npages), cw["pos"], cw["w1"], cw["b1"], cw["w2p"], cw["b2p"])


def _compress_weights(cmp_pos, cmp_w1, cmp_b1, cmp_w2, cmp_b2):
    width = 2 * B_KV_HEADS * B_HEAD_DIM
    npk = CMP_LEN // CMP_PACK
    w2p = jnp.zeros((2 * B_KV_HEADS, CMP_HIDDEN, width), F32)
    for c in range(2 * B_KV_HEADS):
        w2p = w2p.at[c, :, c * B_HEAD_DIM:(c + 1) * B_HEAD_DIM].set(cmp_w2[c // B_KV_HEADS])
    return {
        "pos": cmp_pos.reshape(2, npk, 1, CMP_PACK * B_HEAD_DIM),
        "w1": cmp_w1.reshape(2, npk, CMP_PACK * B_HEAD_DIM, CMP_HIDDEN).astype(BF16),
        "b1": cmp_b1.reshape(2, 1, CMP_HIDDEN),
        "w2p": w2p.astype(BF16),
        "b2p": jnp.tile(cmp_b2[:, None, :], (1, B_KV_HEADS, 1)).reshape(1, width),
    }


def _softmax_masked(s, mask):
    s = jnp.where(mask, s, NEG)
    e = jnp.exp(s - jnp.max(s, axis=1, keepdims=True))
    return jnp.where(mask, e / jnp.sum(e, axis=1, keepdims=True), 0.0)


def _dot_nt(a, b):
    return lax.dot_general(a, b, (((1,), (1,)), ((), ())), preferred_element_type=F32)


def _low_half(tq):
    return lax.broadcasted_iota(jnp.int32, (tq, 128), 1) < B_HEAD_DIM


def _pad_heads(qblk, gl, tq):
    low = _low_half(tq)
    keep = low if gl == 0 else jnp.logical_not(low)
    tiles = []
    for i in range(4):
        hh = gl * 4 + i
        tile = qblk[:, (hh // 2) * 128:(hh // 2 + 1) * 128]
        if hh % 2 != gl:
            tile = pltpu.roll(tile, B_HEAD_DIM, axis=1)
        tiles.append(jnp.where(keep, tile, 0.0))
    return jnp.concatenate(tiles, axis=0).astype(BF16)


def _select_blocks(psum, qpos1, cover, ns):
    hi = psum.astype(BF16)
    lo = (psum - hi.astype(F32)).astype(BF16)
    imp = jnp.dot(hi, cover, preferred_element_type=F32) + jnp.dot(lo, cover, preferred_element_type=F32)
    s_idx = lax.broadcasted_iota(jnp.int32, (1, 128), 1)
    cur = jnp.right_shift(qpos1, 6)
    assert SEL_LEN == 64
    valid = s_idx * SEL_LEN <= qpos1
    forced = (s_idx == 0) | (s_idx == cur) | (s_idx == cur - 1)
    score = jnp.where(valid, imp + jnp.where(forced, FORCE_BONUS, 0.0), NEG)
    score = jnp.where(s_idx < ns, score, 2 * NEG)
    rank = jnp.zeros_like(score)
    for sp in range(ns):
        colv = score[:, sp:sp + 1]
        ahead = (colv > score) | ((colv == score) & (s_idx > sp))
        rank = rank + jnp.where(ahead, 1.0, 0.0)
    return jnp.where((rank < min(SEL_TOPN, ns)) & (s_idx < ns), 1.0, 0.0)


def _select_blocks_t(psum, qpos_row, cover_t, ns):
    tq = psum.shape[0]
    pt = psum.T
    hi = pt.astype(BF16)
    lo = (pt - hi.astype(F32)).astype(BF16)
    imp = jnp.dot(cover_t, hi, preferred_element_type=F32) + jnp.dot(cover_t, lo, preferred_element_type=F32)
    nr = -(-ns // 8) * 8
    imp = imp[0:nr]
    s_idx = lax.broadcasted_iota(jnp.int32, (nr, 1), 0)
    cur = jnp.right_shift(qpos_row, 6)
    valid = s_idx * SEL_LEN <= qpos_row
    forced = (s_idx == 0) | (s_idx == cur) | (s_idx == cur - 1)
    score = jnp.where(valid, imp + jnp.where(forced, FORCE_BONUS, 0.0), NEG)
    score = jnp.where(s_idx < ns, score, 2 * NEG)
    rank = jnp.zeros_like(score)
    for sp in range(ns):
        rowv = score[sp:sp + 1, :]
        ahead = (rowv > score) | ((rowv == score) & (s_idx > sp))
        rank = rank + jnp.where(ahead, 1.0, 0.0)
    sel_t = jnp.where((rank < min(SEL_TOPN, ns)) & (s_idx < ns), 1.0, 0.0)
    return jnp.concatenate([sel_t, jnp.zeros((128 - nr, tq), F32)], axis=0).T


def _cmp_branch(qs, tq, qpos, kc, vc):
    n_idx = lax.broadcasted_iota(jnp.int32, (1, 128), 1)
    p_c = _softmax_masked(_dot_nt(qs, kc), n_idx * CMP_STRIDE + (CMP_LEN - 1) <= qpos)
    o_c = jnp.dot(p_c.astype(BF16), vc, preferred_element_type=F32)
    return o_c, p_c[0:tq] + p_c[tq:2 * tq] + p_c[2 * tq:3 * tq] + p_c[3 * tq:4 * tq]


def _softmax_pv(s, tq, bias, pv):
    nk = s.shape[1]
    s3 = s.reshape(4, tq, nk) + bias[None]
    e = jnp.exp(s3 - jnp.max(s3, axis=2, keepdims=True))
    den = jnp.sum(e, axis=2, keepdims=True)
    o = pv(e.reshape(4 * tq, nk).astype(BF16))
    return o * (1.0 / den.reshape(4 * tq, 1))


def _gate_expand_matrix():
    c = np.arange(128)[:, None]
    col = np.arange(3 * 512)[None, :]
    return jnp.asarray(c == (col // 512) * 8 + (col % 512) // B_HEAD_DIM, BF16)


def _combine(o_heads, sg, zs, tq, gexp):
    low = _low_half(tq)
    hi = sg.astype(BF16)
    lo = (sg - hi.astype(F32)).astype(BF16)
    gate = jnp.dot(hi, gexp, preferred_element_type=F32) + jnp.dot(lo, gexp, preferred_element_type=F32)
    y = jnp.zeros((tq, 512), F32)
    for br in range(3):
        tiles = []
        for j in range(4):
            a = o_heads[br][2 * j]
            b = o_heads[br][2 * j + 1]
            if j // 2 == 0:
                tiles.append(jnp.where(low, a, pltpu.roll(b, B_HEAD_DIM, axis=1)))
            else:
                tiles.append(jnp.where(low, pltpu.roll(a, B_HEAD_DIM, axis=1), b))
        z = zs[br].astype(F32)
        y = y + jnp.concatenate(tiles, axis=1) * gate[:, br * 512:(br + 1) * 512] * (z * _sigmoid(z))
    return y


def _nsa_prompt_kernel(q_ref, z0_ref, z1_ref, z2_ref, gate_ref, kc_ref, vc_ref, ks_ref, vs_ref, kw_ref, vw_ref,
                       cover_t_ref, blockneg_ref, gexp_ref, y_ref, osel_sc, kaug_sc, *, tq, seq, kstep):
    q0 = pl.program_id(2) * tq
    qblk = q_ref[...].astype(F32) * (B_HEAD_DIM ** -0.5)
    qpos1 = q0 + lax.broadcasted_iota(jnp.int32, (tq, 1), 0)
    qpos_row = q0 + lax.broadcasted_iota(jnp.int32, (1, tq), 1)
    qpos = jnp.concatenate([qpos1] * 4, axis=0)
    wlen = WINDOW + tq
    wstart = pl.multiple_of(jnp.maximum(q0 - WINDOW, 0), 128)
    kw = kw_ref[pl.ds(wstart, wlen), :]
    vw = vw_ref[pl.ds(wstart, wlen), :]
    kpos_w = wstart + lax.broadcasted_iota(jnp.int32, (1, wlen), 1)
    wbias = jnp.where((kpos_w <= qpos1) & (kpos_w > qpos1 - WINDOW), 0.0, NEG)
    kc = kc_ref[0].astype(BF16)
    vc = vc_ref[0].astype(BF16)

    @pl.when(pl.program_id(2) == 0)
    def _():
        kaug_sc[:, 0:128] = ks_ref[...]
        kaug_sc[:, 128:256] = blockneg_ref[...]

    nbefore = (q0 + kstep - 1) // kstep
    first_blk = q0 // SEL_LEN
    kd = ks_ref[pl.ds(pl.multiple_of(q0, 128), tq), :]
    vd = vs_ref[pl.ds(pl.multiple_of(q0, 128), tq), :]
    tri = jnp.where(lax.broadcasted_iota(jnp.int32, (tq, tq), 1) <= lax.broadcasted_iota(jnp.int32, (tq, tq), 0),
                    0.0, NEG)
    s_idx = lax.broadcasted_iota(jnp.int32, (1, 128), 1)
    phase1 = []
    for gl in range(2):
        qs = _pad_heads(qblk, gl, tq)
        o_c, psum = _cmp_branch(qs, tq, qpos, kc, vc)
        sel = _select_blocks_t(psum, qpos_row, cover_t_ref[...], seq // SEL_LEN)
        skip = jnp.where((sel > 0.5) & (s_idx < first_blk), 0.0, 1.0).astype(BF16)
        q_aug = jnp.concatenate([qs, jnp.concatenate([skip] * 4, axis=0)], axis=1)
        s_d = _dot_nt(qs, kd).reshape(4, tq, tq) + tri[None]
        m_d = jnp.max(s_d, axis=2, keepdims=True)
        o_w = _softmax_pv(_dot_nt(qs, kw), tq, wbias, lambda e: jnp.dot(e, vw, preferred_element_type=F32))
        phase1.append((o_c, o_w, q_aug, s_d, m_d))

    o_heads = [[None] * 8 for _ in range(3)]
    for gl in range(2):
        o_c, o_w, q_aug, s_d, m_d = phase1[gl]

        @pl.when(nbefore == 0)
        def _(gl=gl, s_d=s_d, m_d=m_d):
            e_d = jnp.exp(s_d - m_d)
            den = jnp.sum(e_d, axis=2, keepdims=True)
            o = jnp.dot(e_d.reshape(4 * tq, tq).astype(BF16), vd, preferred_element_type=F32)
            osel_sc[gl] = o * (1.0 / den.reshape(4 * tq, 1))

        for v in range(1, seq // kstep + 1):
            @pl.when(nbefore == v)
            def _(nk=v * kstep, gl=gl, q_aug=q_aug, s_d=s_d, m_d=m_d):
                s_b = _dot_nt(q_aug, kaug_sc[0:nk, :]).reshape(4, tq, nk)
                m = jnp.maximum(m_d, jnp.max(s_b, axis=2, keepdims=True))
                e_b = jnp.exp(s_b - m)
                e_d = jnp.exp(s_d - m)
                den = jnp.sum(e_b, axis=2, keepdims=True) + jnp.sum(e_d, axis=2, keepdims=True)
                o = (jnp.dot(e_b.reshape(4 * tq, nk).astype(BF16), vs_ref[0:nk, :], preferred_element_type=F32)
                     + jnp.dot(e_d.reshape(4 * tq, tq).astype(BF16), vd, preferred_element_type=F32))
                osel_sc[gl] = o * (1.0 / den.reshape(4 * tq, 1))

    for gl in range(2):
        outs = (phase1[gl][0], osel_sc[gl], phase1[gl][1])
        for br in range(3):
            for i in range(4):
                o_heads[br][gl * 4 + i] = outs[br][i * tq:(i + 1) * tq]
    y = _combine(o_heads, _sigmoid(gate_ref[...]), (z0_ref[...], z1_ref[...], z2_ref[...]), tq, gexp_ref[...])
    y_ref[...] = y.astype(BF16)


def _cover_matrix(transposed=False):
    n = np.arange(128)[:, None]
    s = np.arange(128)[None, :]
    cov = (n * CMP_STRIDE < s * SEL_LEN + SEL_LEN) & (n * CMP_STRIDE + CMP_LEN - 1 >= s * SEL_LEN)
    return jnp.asarray(cov.T if transposed else cov, BF16)


def _expand_matrix(nkeys):
    s = np.arange(128)[:, None]
    key = np.arange(nkeys)[None, :]
    return jnp.asarray(key // SEL_LEN == s, BF16)


def nsa_attend_prompt(pm, pg, cmp, kvb, nb, seq):
    tq = 128
    nq = seq // tq

    def rows(width, col):
        return pl.BlockSpec((tq, width), lambda b, gp, t: (b * nq + t, col(gp)))

    def seq_block(first):
        return pl.BlockSpec((seq, 128), lambda b, gp, t: (b, first + gp))

    kstep = 256
    assert seq % kstep == 0 and tq % 128 == 0
    return pl.pallas_call(
        functools.partial(_nsa_prompt_kernel, tq=tq, seq=seq, kstep=kstep),
        grid=(nb, 2, nq),
        in_specs=[rows(512, lambda gp: gp), rows(512, lambda gp: 2 + gp), rows(512, lambda gp: 4 + gp),
                  rows(512, lambda gp: 6 + gp), rows(128, lambda gp: gp),
                  pl.BlockSpec((1, 128, 128), lambda b, gp, t: (b, 0, gp)),
                  pl.BlockSpec((1, 128, 128), lambda b, gp, t: (b, 0, 2 + gp)),
                  seq_block(4), seq_block(6), seq_block(8), seq_block(10),
                  pl.BlockSpec((128, 128), lambda b, gp, t: (0, 0)),
                  pl.BlockSpec((seq, 128), lambda b, gp, t: (0, 0)),
                  pl.BlockSpec((128, 3 * 512), lambda b, gp, t: (0, 0))],
        out_specs=rows(512, lambda gp: gp),
        out_shape=jax.ShapeDtypeStruct((nb * seq, B_ATT), BF16),
        scratch_shapes=[pltpu.VMEM((2, 4 * tq, 128), F32), pltpu.VMEM((seq, 256), BF16)],
        compiler_params=_params(("parallel", "parallel", "arbitrary"), 56),
        name="nsa_attend_prompt",
    )(pm, pm, pm, pm, pg, cmp, cmp, kvb, kvb, kvb, kvb, _cover_matrix(transposed=True),
      NEG * _expand_matrix(seq).T, _gate_expand_matrix())


def _nsa_sample_kernel(pt_ref, *refs, npages, past):
    del pt_ref
    pages = refs[:npages]
    (q_ref, z0_ref, z1_ref, z2_ref, gate_ref, cmp_ref, new_ref, win_ref, cover_ref, expand_ref, gexp_ref, y_ref,
     kst_sc, vst_sc) = refs[npages:]
    tq = SPAD
    ngp = 2
    rows_gp = 8 * tq
    wrows = win_ref.shape[2]
    ns = past // SEL_LEN + 1
    sg = _sigmoid(gate_ref[...])

    def qpos_rows(n):
        return past + jnp.bitwise_and(lax.broadcasted_iota(jnp.int32, (n, 1), 0), tq - 1)

    def new_rows(col):
        blk = new_ref[:, col:col + 128].astype(BF16)
        return jnp.concatenate([blk, jnp.zeros((NEW_PAD - tq, 128), BF16)], axis=0)

    def per_pair(fn):
        return jnp.concatenate([fn(gp) for gp in range(ngp)], axis=0)

    for gp in range(ngp):
        lo, hi = gp * 128, (gp + 1) * 128
        for p in range(npages):
            kst_sc[gp, :, p * PAGE:(p + 1) * PAGE] = pages[p][0, lo:hi, :].astype(BF16)
            vst_sc[gp, :, p * PAGE:(p + 1) * PAGE] = pages[p][0, 256 + lo:256 + hi, :].astype(BF16)
    qs = []
    for gp in range(ngp):
        qblk = q_ref[:, gp * 512:(gp + 1) * 512].astype(F32) * (B_HEAD_DIM ** -0.5)
        qs.append(jnp.concatenate([_pad_heads(qblk, 0, tq), _pad_heads(qblk, 1, tq)], axis=0))

    n_idx = lax.broadcasted_iota(jnp.int32, (1, 128), 1)
    s = per_pair(lambda gp: _dot_nt(qs[gp], cmp_ref[0, :, gp * 128:(gp + 1) * 128].astype(BF16)))
    p_c = _softmax_masked(s, n_idx * CMP_STRIDE + (CMP_LEN - 1) <= qpos_rows(ngp * rows_gp))
    p_cb = p_c.astype(BF16)
    o_c = [jnp.dot(p_cb[gp * rows_gp:(gp + 1) * rows_gp],
                   cmp_ref[0, :, 256 + gp * 128:256 + (gp + 1) * 128].astype(BF16), preferred_element_type=F32)
           for gp in range(ngp)]
    psum = jnp.concatenate([sum(p_c[(g * 4 + i) * tq:(g * 4 + i + 1) * tq] for i in range(4))
                            for g in range(2 * ngp)], axis=0)
    nsel_rows = 2 * ngp * tq
    sel = _select_blocks(psum, qpos_rows(nsel_rows), cover_ref[...], ns).astype(BF16)
    selk = jnp.dot(sel, expand_ref[...], preferred_element_type=F32) > 0.5
    kpos_s = lax.broadcasted_iota(jnp.int32, (1, past + NEW_PAD), 1)
    sbias = jnp.where(selk & (kpos_s <= qpos_rows(nsel_rows)), 0.0, NEG)
    sbias = jnp.concatenate([sbias[g * tq:(g + 1) * tq] for g in range(2 * ngp) for _ in range(4)], axis=0)
    s = per_pair(lambda gp: jnp.concatenate(
        [jnp.dot(qs[gp], kst_sc[gp], preferred_element_type=F32),
         _dot_nt(qs[gp], new_rows(512 + gp * 128))], axis=1)) + sbias
    e = jnp.exp(s - jnp.max(s, axis=1, keepdims=True))
    rden = 1.0 / jnp.sum(e, axis=1, keepdims=True)
    e = e.astype(BF16)
    o_s = [(_dot_nt(e[gp * rows_gp:(gp + 1) * rows_gp, 0:past], vst_sc[gp])
            + jnp.dot(e[gp * rows_gp:(gp + 1) * rows_gp, past:], new_rows(768 + gp * 128),
                      preferred_element_type=F32)) * rden[gp * rows_gp:(gp + 1) * rows_gp]
           for gp in range(ngp)]
    kpos_w = (past - wrows) + lax.broadcasted_iota(jnp.int32, (1, wrows + NEW_PAD), 1)
    qpos = qpos_rows(ngp * rows_gp)
    wbias = jnp.where((kpos_w <= qpos) & (kpos_w > qpos - WINDOW) & (kpos_w >= 0), 0.0, NEG)
    s = per_pair(lambda gp: jnp.concatenate(
        [jnp.dot(qs[gp], win_ref[0, gp * 128:(gp + 1) * 128, :].astype(BF16), preferred_element_type=F32),
         _dot_nt(qs[gp], new_rows(1024 + gp * 128))], axis=1)) + wbias
    e = jnp.exp(s - jnp.max(s, axis=1, keepdims=True))
    rden = 1.0 / jnp.sum(e, axis=1, keepdims=True)
    e = e.astype(BF16)
    o_w = [(_dot_nt(e[gp * rows_gp:(gp + 1) * rows_gp, 0:wrows],
                    win_ref[0, 256 + gp * 128:256 + (gp + 1) * 128, :].astype(BF16))
            + jnp.dot(e[gp * rows_gp:(gp + 1) * rows_gp, wrows:], new_rows(1280 + gp * 128),
                      preferred_element_type=F32)) * rden[gp * rows_gp:(gp + 1) * rows_gp]
           for gp in range(ngp)]

    for gp in range(ngp):
        o_heads = [[outs[gp][hh * tq:(hh + 1) * tq] for hh in range(8)] for outs in (o_c, o_s, o_w)]
        zs = (z0_ref[:, gp * 512:(gp + 1) * 512], z1_ref[:, gp * 512:(gp + 1) * 512],
              z2_ref[:, gp * 512:(gp + 1) * 512])
        y_ref[:, gp * 512:(gp + 1) * 512] = _combine(o_heads, sg[:, gp * 128:(gp + 1) * 128], zs, tq,
                                                      gexp_ref[...]).astype(BF16)


def nsa_attend_sample(pm, pg, cmp, kv_new, cache_pages, table, cache_win, nb, past):
    npages = past // PAGE
    wrows = cache_win.shape[2]
    nks = past + NEW_PAD
    assert past % SEL_LEN == 0 and SPAD <= SEL_LEN

    def page_map(b, pt, p):
        return (pt[b * npages + p], 1, 0)

    def rows(width, col):
        return pl.BlockSpec((SPAD, width), lambda b, pt: (b, col))

    grid_spec = pltpu.PrefetchScalarGridSpec(
        num_scalar_prefetch=1,
        grid=(nb,),
        in_specs=[pl.BlockSpec((1, 512, PAGE), functools.partial(page_map, p=p)) for p in range(npages)]
        + [rows(B_ATT, 0), rows(B_ATT, 1), rows(B_ATT, 2), rows(B_ATT, 3), rows(256, 0),
           pl.BlockSpec((1, 128, 512), lambda b, pt: (b, 0, 0)),
           rows(1536, 0),
           pl.BlockSpec((1, 512, wrows), lambda b, pt: (b, 0, 0)),
           pl.BlockSpec((128, 128), lambda b, pt: (0, 0)),
           pl.BlockSpec((128, nks), lambda b, pt: (0, 0)),
           pl.BlockSpec((128, 3 * 512), lambda b, pt: (0, 0))],
        out_specs=rows(B_ATT, 0),
        scratch_shapes=[pltpu.VMEM((2, 128, past), BF16), pltpu.VMEM((2, 128, past), BF16)],
    )
    return pl.pallas_call(
        functools.partial(_nsa_sample_kernel, npages=npages, past=past),
        grid_spec=grid_spec,
        out_shape=jax.ShapeDtypeStruct((nb * SPAD, B_ATT), BF16),
        compiler_params=_params(("arbitrary",), 48),
        name="nsa_attend_sample",
    )(table, *([cache_pages] * npages), pm, pm, pm, pm, pg, cmp, kv_new, cache_win,
      _cover_matrix(), _expand_matrix(nks), _gate_expand_matrix())


def _nsa_weights(lb, b_norm_pre, b_w_in, b_w_out, b_norm_post):
    w_in = b_w_in[lb]
    wg = w_in[:, 4 * B_ATT:].reshape(D_MODEL, 3, 2, 8)
    wg = jnp.transpose(wg, (0, 2, 1, 3)).reshape(D_MODEL, 2, 24)
    wg = jnp.pad(wg, ((0, 0), (0, 0), (0, 128 - 24))).reshape(D_MODEL, 256)
    return {"norm_pre": b_norm_pre[lb], "w_main": w_in[:, :4 * B_ATT].astype(BF16), "w_gate": wg.astype(BF16),
            "w_out": b_w_out[lb].astype(BF16), "norm_post": b_norm_post[lb]}


def kernel(x_prompt, x_sample, cache_kv, page_table, cache_win, state_C, state_n, state_m, state_conv,
           a_norm_pre, a_w_up, a_conv_w, a_conv_b, a_w_q, a_w_k, a_w_v, a_w_if, a_b_if,
           a_mh_w, a_skip, a_w_down, a_norm_post,
           kv_norm, w_kv, cmp_pos, cmp_w1, cmp_b1, cmp_w2, cmp_b2,
           b_norm_pre, b_w_in, b_w_out, b_norm_post):
    bp, sp = x_prompt.shape[0], x_prompt.shape[1]
    bs, ds = x_sample.shape[0], x_sample.shape[1]
    n_a = state_C.shape[0]
    n_b = b_w_in.shape[0]
    past = page_table.shape[1] * cache_kv.shape[1]
    assert cache_kv.shape[1] == PAGE and sp % PAGE == 0 and ds <= SPAD
    assert (past + ds - CMP_LEN) // CMP_STRIDE + 1 == past // CMP_STRIDE - 1
    xp = x_prompt.reshape(bp * sp, D_MODEL)
    xs = jnp.pad(x_sample, ((0, 0), (0, SPAD - ds), (0, 0))).reshape(bs * SPAD, D_MODEL)

    np_l, ns_l, mp_l, ms_l, vp_l, vs_l = [], [], [], [], [], []
    c_prompt = c_sample = None
    for layer in range(n_a):
        lw = _mlstm_weights(layer, a_norm_pre, a_w_up, a_conv_w, a_conv_b, a_w_q, a_w_k, a_w_v, a_w_if, a_b_if,
                            a_mh_w, a_skip, a_w_down, a_norm_post)
        xp, c_prompt, n1, m1, v1 = mlstm_layer_prompt(xp, lw, bp, sp, layer, n_a, c_prompt)
        xs, c_sample, n2, m2, v2 = mlstm_layer_sample(xs, lw, bs, ds, layer, (state_C, state_n, state_m),
                                                      state_conv[layer], c_sample)
        np_l.append(n1); mp_l.append(m1); vp_l.append(v1)
        ns_l.append(n2); ms_l.append(m2); vs_l.append(v2)

    nkv = w_kv.shape[1]
    w_kv_b = w_kv.astype(BF16)
    kvp, kvp_b = norm_matmul(xp, kv_norm, w_kv_b, TM_PROJ, 512, also_bf16=True)
    kvs = norm_matmul(xs, kv_norm, w_kv_b, TM_PROJ, 512)
    cw = _compress_weights(cmp_pos, cmp_w1, cmp_b1, cmp_w2, cmp_b2)
    table_p = jnp.arange(bp * (sp // PAGE), dtype=jnp.int32)
    table_s = page_table.reshape(-1).astype(jnp.int32)
    cache_pages = jnp.transpose(cache_kv, (0, 2, 3, 4, 1)).reshape(cache_kv.shape[0], -1, PAGE)
    cmp_p = compress(kvp.reshape(bp * sp // PAGE, PAGE, nkv), table_p, bp, cw, feature_major=False)
    cmp_s = compress(cache_pages, table_s, bs, cw, feature_major=True)
    wb = cache_win.shape[1]
    win2 = jnp.transpose(cache_win, (0, 2, 3, 4, 1)).reshape(bs, -1, wb)

    for lb in range(n_b):
        bw = _nsa_weights(lb, b_norm_pre, b_w_in, b_w_out, b_norm_post)
        pm = norm_matmul(xp, bw["norm_pre"], bw["w_main"], TM_PROJ, 1024, out_dtype=BF16)
        pg = norm_matmul(xp, bw["norm_pre"], bw["w_gate"], TM_PROJ, 256)
        yp = nsa_attend_prompt(pm, pg, cmp_p, kvp_b, bp, sp)
        xp = matmul_norm_res(yp, bw["w_out"], bw["norm_post"], xp, 512)
        pm = norm_matmul(xs, bw["norm_pre"], bw["w_main"], TM_PROJ, 1024, out_dtype=BF16)
        pg = norm_matmul(xs, bw["norm_pre"], bw["w_gate"], TM_PROJ, 256)
        ys = nsa_attend_sample(pm, pg, cmp_s, kvs, cache_pages, table_s, win2, bs, past)
        xs = matmul_norm_res(ys, bw["w_out"], bw["norm_post"], xs, 512)

    y_prompt = xp.reshape(bp, sp, D_MODEL)
    y_sample = xs.reshape(bs, SPAD, D_MODEL)[:, :ds]
    kvp3 = kvp.reshape(bp, sp, nkv)
    kvs3 = kvs.reshape(bs, SPAD, nkv)[:, :ds]
    ncache = 4 * B_KV_HEADS * B_HEAD_DIM
    kv_rows_prompt = kvp3[:, :, :ncache].reshape(bp, sp, 4, B_KV_HEADS, B_HEAD_DIM)
    kv_rows_sample = kvs3[:, :, :ncache].reshape(bs, ds, 4, B_KV_HEADS, B_HEAD_DIM)
    win_prompt = kvp3[:, sp - min(WINDOW, sp):, ncache:].reshape(bp, min(WINDOW, sp), 2, B_KV_HEADS, B_HEAD_DIM)
    win_new = kvs3[:, :, ncache:].reshape(bs, ds, 2, B_KV_HEADS, B_HEAD_DIM).astype(cache_win.dtype)
    win_sample = jnp.concatenate([cache_win, win_new], axis=1)[:, ds:]
    return (y_prompt, y_sample, kv_rows_prompt, kv_rows_sample, win_prompt, win_sample,
            c_prompt, c_sample, jnp.stack(np_l), jnp.stack(ns_l),
            jnp.stack(mp_l), jnp.stack(ms_l), jnp.stack(vp_l), jnp.stack(vs_l))
```
